```python
import math
import jax
import jax.numpy as jnp
from jax import lax
import numpy as np

D_MODEL = 1024
BATCH = 2
SEQ = 8192
DEPTH = 4

HEAD_DIM = 64
GROUP_HEADS = 4
GROUP_WIDTH = GROUP_HEADS * HEAD_DIM
D_MIX = 4 * GROUP_WIDTH
GLA_RANK = 16
GLA_TAU = 16.0
GLA_CHUNK = 64
GRID_W = 64
NA_ROWS_MAX = 8
NA_COLS = 16
LRU_CONV = 4
LRU_CONV_LEFT = 2
LRU_C = 8.0
DIL_PAIRS = ((128, 1), (512, 4), (2048, 16))
ROPE_THETA = 10000.0
D_FF = -(-8 * D_MODEL // (3 * 256)) * 256
EPS = 1e-6
SPLITS = (GROUP_WIDTH,) * 4 + (2 * GLA_RANK,) + (GROUP_WIDTH,) * 3 + (GROUP_WIDTH,) * 2 + (GROUP_WIDTH,) * 3
D_IN = sum(SPLITS)

kernel_name = "hybrid_parallel_head_group_encoder"


def rmsnorm(x, g):
    xf = x.astype(jnp.float32)
    y = xf * lax.rsqrt(jnp.mean(xf * xf, axis=-1, keepdims=True) + EPS)
    return (y * g.astype(jnp.float32)).astype(x.dtype)


def to_heads(t):
    B, L, _ = t.shape
    return t.reshape(B, L, -1, HEAD_DIM).transpose(0, 2, 1, 3)


def from_heads(t):
    B, H, L, dh = t.shape
    return t.transpose(0, 2, 1, 3).reshape(B, L, H * dh)


def rope(t, cos, sin):
    t1, t2 = jnp.split(t, 2, axis=-1)
    c = cos.astype(t.dtype)
    s = sin.astype(t.dtype)
    return jnp.concatenate([t1 * c - t2 * s, t2 * c + t1 * s], axis=-1)


def gla_chunked(q, k, v, log_a):
    B, H, L, dk = q.shape
    dv = v.shape[-1]
    C = GLA_CHUNK
    n = L // C
    q, k, v, log_a = [t.reshape(B, H, n, C, -1) for t in (q, k, v, log_a)]
    b = jnp.cumsum(log_a, axis=3)
    b_last = b[:, :, :, C - 1:C, :]
    b_mid = b[:, :, :, C // 2 - 1:C // 2, :]
    att = jnp.einsum('bhnck,bhnsk->bhncs', q * jnp.exp(b - b_mid), k * jnp.exp(b_mid - b))
    tri = np.tril(np.ones((C, C), dtype=bool))
    att = jnp.where(tri, att, 0.0)
    o_intra = jnp.einsum('bhncs,bhnsv->bhncv', att, v)
    chunk_kv = jnp.einsum('bhnck,bhncv->bhnkv', k * jnp.exp(b_last - b), v)
    decay = jnp.exp(b_last[:, :, :, 0, :])

    def step(S, inp):
        kv_c, d_c = inp
        return S * d_c[..., None] + kv_c, S

    S0 = jnp.zeros((B, H, dk, dv), jnp.float32)
    _, S_prev = lax.scan(step, S0, (jnp.moveaxis(chunk_kv, 2, 0), jnp.moveaxis(decay, 2, 0)))
    S_prev = jnp.moveaxis(S_prev, 0, 2)
    o_inter = jnp.einsum('bhnck,bhnkv->bhncv', q * jnp.exp(b), S_prev)
    return (o_intra + o_inter).reshape(B, H, L, dv)


def gla_mixer(q, k, v, g, z, w_gate, b_gate, norm_g):
    B, L, _ = q.shape
    f32 = jnp.float32
    zl = z.astype(f32).reshape(B, L, 2, GLA_RANK)
    logit = jnp.einsum('bler,erc->eblc', zl, w_gate.astype(f32)) + b_gate.astype(f32)[:, None, None, :]
    log_a = jax.nn.log_sigmoid(logit) / GLA_TAU
    qh = to_heads(q).astype(f32) * (HEAD_DIM ** -0.5)
    kh = to_heads(k).astype(f32)
    vh = to_heads(v).astype(f32)
    flip = lambda t: jnp.flip(t, axis=2)
    o_f = gla_chunked(qh, kh, vh, to_heads(log_a[0]))
    o_b = flip(gla_chunked(flip(qh), flip(kh), flip(vh), flip(to_heads(log_a[1]))))
    o = o_f + o_b
    o = o * lax.rsqrt(jnp.mean(o * o, axis=-1, keepdims=True) + EPS)
    o = o * norm_g.astype(f32).reshape(GROUP_HEADS, 1, HEAD_DIM)
    return (from_heads(o) * jax.nn.silu(g.astype(f32))).astype(q.dtype)


def neighbourhood_attention(q, k, v, rpb):
    B, H, L, dh = q.shape
    rows = L // GRID_W
    kr = min(NA_ROWS_MAX, rows)
    grid = lambda t: t.reshape(B, H, rows, GRID_W, dh)
    qg, kg, vg = grid(q), grid(k), grid(v)
    r = np.arange(rows)
    row_idx = np.clip(r - kr // 2, 0, rows - kr)[:, None] + np.arange(kr)[None, :]
    k_rows = kg[:, :, row_idx]
    v_rows = vg[:, :, row_idx]
    c = np.arange(GRID_W)
    col_start = np.clip(c - NA_COLS // 2, 0, GRID_W - NA_COLS)
    col_ok = (c[None, :] >= col_start[:, None]) & (c[None, :] < col_start[:, None] + NA_COLS)
    dr = row_idx - r[:, None]
    dc = np.clip(c[None, :] - c[:, None], -(NA_COLS - 1), NA_COLS - 1)
    bias = rpb[:, (dr + NA_ROWS_MAX - 1)[:, None, :, None], (dc + NA_COLS - 1)[None, :, None, :]]
    s = jnp.einsum('bhrqd,bhrikd->bhrqik', qg, k_rows).astype(jnp.float32) * (dh ** -0.5)
    s = s + bias.astype(jnp.float32)
    s = jnp.where(col_ok[:, None, :], s, -jnp.inf)
    p = jax.nn.softmax(s.reshape(B, H, rows, GRID_W, kr * GRID_W), axis=-1).reshape(s.shape)
    o = jnp.einsum('bhrqik,bhrikd->bhrqd', p.astype(v.dtype), v_rows)
    return o.reshape(B, H, L, dh)


def linear_scan(a, u):
    def combine(left, right):
        a_l, u_l = left
        a_r, u_r = right
        return a_l * a_r, a_r * u_l + u_r
    return lax.associative_scan(combine, (a, u), axis=1)[1]


def rglru_mixer(xb, gate, conv_w, conv_b, w_a, b_a, w_x, b_x, lam):
    B, L, C = xb.shape
    f32 = jnp.float32
    xp = jnp.pad(xb.astype(f32), ((0, 0), (LRU_CONV_LEFT, LRU_CONV - 1 - LRU_CONV_LEFT), (0, 0)))
    xc = conv_b.astype(f32)
    for j in range(LRU_CONV):
        xc = xc + xp[:, j:j + L, :] * conv_w[j].astype(f32)
    xh = xc.reshape(B, L, GROUP_HEADS, HEAD_DIM)
    r = jax.nn.sigmoid(jnp.einsum('blhi,ehij->eblhj', xh, w_a.astype(f32)).reshape(2, B, L, C)
                       + b_a.astype(f32)[:, None, None, :])
    i = jax.nn.sigmoid(jnp.einsum('blhi,ehij->eblhj', xh, w_x.astype(f32)).reshape(2, B, L, C)
                       + b_x.astype(f32)[:, None, None, :])
    log_a = -LRU_C * r * jax.nn.softplus(-lam.astype(f32))[:, None, None, :]
    a = jnp.exp(log_a)
    u = jnp.sqrt(-jnp.expm1(2.0 * log_a)) * (i * xc[None])
    flip = lambda t: jnp.flip(t, axis=1)
    h = linear_scan(a[0], u[0]) + flip(linear_scan(flip(a[1]), flip(u[1])))
    return (h * jax.nn.gelu(gate.astype(f32))).astype(xb.dtype)


def band_attention(q, k, v, radius):
    lead = q.shape[:-2]
    n, dh = q.shape[-2], q.shape[-1]
    Q = radius
    nb = -(-n // Q)
    n_pad = nb * Q
    nl = len(lead)
    qb = jnp.pad(q, ((0, 0),) * nl + ((0, n_pad - n), (0, 0))).reshape(lead + (nb, Q, dh))
    padkv = lambda t: jnp.pad(t, ((0, 0),) * nl + ((Q, n_pad - n + Q), (0, 0))).reshape(lead + (nb + 2, Q, dh))
    kp, vp = padkv(k), padkv(v)
    band = lambda t: jnp.concatenate([t[..., 0:nb, :, :], t[..., 1:nb + 1, :, :], t[..., 2:nb + 2, :, :]], axis=-2)
    kb, vb = band(kp), band(vp)
    blk = np.arange(nb)[:, None, None]
    qpos = blk * Q + np.arange(Q)[None, :, None]
    kpos = blk * Q + np.arange(3 * Q)[None, None, :] - Q
    valid = (np.abs(kpos - qpos) <= radius) & (kpos >= 0) & (kpos < n)
    s = jnp.einsum('...bqd,...bkd->...bqk', qb, kb).astype(jnp.float32) * (dh ** -0.5)
    s = jnp.where(valid, s, -jnp.inf)
    m = jnp.max(s, axis=-1, keepdims=True)
    e = jnp.exp(s - m)
    den = jnp.sum(e, axis=-1, keepdims=True)
    o = jnp.einsum('...bqk,...bkd->...bqd', (e / den).astype(v.dtype), vb)
    lse = (m + jnp.log(den))[..., 0]
    o = o.reshape(lead + (n_pad, dh))[..., :n, :]
    lse = lse.reshape(lead + (n_pad,))[..., :n]
    return o, lse


def dilated_attention(q, k, v):
    B, H, L, dh = q.shape
    outs, lses = [], []
    for window, dil in DIL_PAIRS:
        radius = window // (2 * dil)
        n = L // dil
        sub = lambda t: t.reshape(B, H, n, dil, dh).transpose(0, 1, 3, 2, 4)
        o, lse = band_attention(sub(q), sub(k), sub(v), radius)
        outs.append(o.transpose(0, 1, 3, 2, 4).reshape(B, H, L, dh).astype(jnp.float32))
        lses.append(lse.transpose(0, 1, 3, 2).reshape(B, H, L))
    wts = jax.nn.softmax(jnp.stack(lses, axis=0), axis=0)
    return jnp.einsum('gbhl,gbhld->bhld', wts, jnp.stack(outs, axis=0)).astype(q.dtype)


def setup_inputs(seed: int = 0) -> dict:
    key = jax.random.key(seed)
    ks = jax.random.split(key, 24)
    f32 = jnp.float32
    nrm = lambda k, shape, scale: scale * jax.random.normal(k, shape, f32)
    gain = lambda k, d: 1.0 + 0.02 * jax.random.normal(k, (DEPTH, d), f32)
    u = jax.random.uniform(ks[14], (DEPTH, 2, GROUP_WIDTH), f32, 0.9, 0.999)
    return {
        "x": jax.random.normal(ks[0], (BATCH, SEQ, D_MODEL), f32),
        "mix_norm_pre": gain(ks[1], D_MODEL),
        "mix_norm_post": gain(ks[2], D_MODEL),
        "w_in": nrm(ks[3], (DEPTH, D_MODEL, D_IN), D_MODEL ** -0.5),
        "gla_w_gate": nrm(ks[4], (DEPTH, 2, GLA_RANK, GROUP_WIDTH), GLA_RANK ** -0.5),
        "gla_b_gate": nrm(ks[5], (DEPTH, 2, GROUP_WIDTH), 0.1),
        "gla_norm": gain(ks[6], GROUP_WIDTH),
        "na_rpb": nrm(ks[7], (DEPTH, GROUP_HEADS, 2 * NA_ROWS_MAX - 1, 2 * NA_COLS - 1), 0.1),
        "lru_conv_w": nrm(ks[8], (DEPTH, LRU_CONV, GROUP_WIDTH), LRU_CONV ** -0.5),
        "lru_conv_b": nrm(ks[9], (DEPTH, GROUP_WIDTH), 0.02),
        "lru_w_a": nrm(ks[10], (DEPTH, 2, GROUP_HEADS, HEAD_DIM, HEAD_DIM), HEAD_DIM ** -0.5),
        "lru_b_a": nrm(ks[11], (DEPTH, 2, GROUP_WIDTH), 0.1),
        "lru_w_x": nrm(ks[12], (DEPTH, 2, GROUP_HEADS, HEAD_DIM, HEAD_DIM), HEAD_DIM ** -0.5),
        "lru_b_x": nrm(ks[13], (DEPTH, 2, GROUP_WIDTH), 0.1),
        "lru_lambda": jnp.log(u) - jnp.log1p(-u),
        "w_out": nrm(ks[15], (DEPTH, D_MIX, D_MODEL), D_MIX ** -0.5),
        "ffn_norm_pre": gain(ks[16], D_MODEL),
        "ffn_norm_post": gain(ks[17], D_MODEL),
        "ffn_w_in": nrm(ks[18], (DEPTH, D_MODEL, 2 * D_FF), D_MODEL ** -0.5),
        "ffn_w_out": nrm(ks[19], (DEPTH, D_FF, D_MODEL), D_FF ** -0.5),
    }


def reference(x, mix_norm_pre, mix_norm_post, w_in, gla_w_gate, gla_b_gate, gla_norm, na_rpb,
              lru_conv_w, lru_conv_b, lru_w_a, lru_b_a, lru_w_x, lru_b_x, lru_lambda, w_out,
              ffn_norm_pre, ffn_norm_post, ffn_w_in, ffn_w_out):
    B, L, _ = x.shape
    pos = jnp.arange(L, dtype=jnp.float32)
    inv_freq = ROPE_THETA ** (-jnp.arange(0, HEAD_DIM, 2, dtype=jnp.float32) / HEAD_DIM)
    ang = pos[:, None] * inv_freq[None, :]
    cos, sin = jnp.cos(ang), jnp.sin(ang)
    split_at = [int(s) for s in np.cumsum(SPLITS)[:-1]]
    for l in range(DEPTH):
        h = rmsnorm(x, mix_norm_pre[l])
        p = h @ w_in[l]
        qa, ka, va, ga, za, qb, kb, vb, xc, gc, qd, kd, vd = jnp.split(p, split_at, axis=-1)
        ya = gla_mixer(qa, ka, va, ga, za, gla_w_gate[l], gla_b_gate[l], gla_norm[l])
        yb = from_heads(neighbourhood_attention(to_heads(qb), to_heads(kb), to_heads(vb), na_rpb[l]))
        yc = rglru_mixer(xc, gc, lru_conv_w[l], lru_conv_b[l], lru_w_a[l], lru_b_a[l],
                         lru_w_x[l], lru_b_x[l], lru_lambda[l])
        yd = from_heads(dilated_attention(rope(to_heads(qd), cos, sin), rope(to_heads(kd), cos, sin),
                                          to_heads(vd)))
        y = jnp.concatenate([ya, yb.astype(x.dtype), yc, yd.astype(x.dtype)], axis=-1) @ w_out[l]
        x = x + rmsnorm(y, mix_norm_post[l])
        h = rmsnorm(x, ffn_norm_pre[l])
        gate, up = jnp.split(h @ ffn_w_in[l], 2, axis=-1)
        f = (jax.nn.silu(gate) * up) @ ffn_w_out[l]
        x = x + rmsnorm(f, ffn_norm_post[l])
    return x
```

```python
import functools

import numpy as np
import jax
import jax.numpy as jnp
from jax import lax
from jax.experimental import pallas as pl
from jax.experimental.pallas import tpu as pltpu

F32 = jnp.float32
BF16 = jnp.bfloat16

D_MODEL = 1024
HEAD_DIM = 64
HEADS = 4
GW = HEADS * HEAD_DIM
GLA_RANK = 16
GLA_TAU = 16.0
GLA_CHUNK = 64
GRID_W = 64
NA_ROWS = 8
NA_COLS = 16
LRU_C = 8.0
DIL_PAIRS = ((128, 1), (512, 4), (2048, 16))
DIL_RADIUS = 64
ROPE_THETA = 10000.0
D_FF = 2816
EPS = 1e-6
QK_SCALE = HEAD_DIM ** -0.5

CB_QB, CB_KB, CB_VB, CB_QD, CB_KD, CB_VD = 0, 1, 2, 3, 4, 5
CB_QA, CB_KA, CB_VA, CB_GA, CB_XC, CB_GC = 6, 7, 8, 9, 10, 11
Z_COL0 = 12 * GW
Z_PAD = 128
P_COLS = Z_COL0 + Z_PAD
P16_COLS = 6 * GW

VMEM_LIMIT = 56 * 1024 * 1024


def _params(sem, vmem=VMEM_LIMIT):
    return pltpu.CompilerParams(dimension_semantics=sem, vmem_limit_bytes=vmem)


def _head_mask(rows, h):
    lane = lax.broadcasted_iota(jnp.int32, (rows, GW), 1)
    return (lane >> 6) == h


def _stack_heads(t):
    rows = t.shape[0]
    zero = jnp.zeros_like(t)
    return jnp.concatenate([jnp.where(_head_mask(rows, h), t, zero) for h in range(HEADS)], axis=0)


def _unstack_heads(t):
    rows = t.shape[0] // HEADS
    out = jnp.where(_head_mask(rows, 0), t[0:rows], 0.0)
    for h in range(1, HEADS):
        out = out + jnp.where(_head_mask(rows, h), t[h * rows:(h + 1) * rows], 0.0)
    return out


def _dot(a, b):
    return jnp.dot(a, b, preferred_element_type=F32)


def _dot_nt(a, b):
    return lax.dot_general(a, b, (((1,), (1,)), ((), ())), preferred_element_type=F32)


def _dot_tn(a, b):
    return lax.dot_general(a, b, (((0,), (0,)), ((), ())), preferred_element_type=F32)


def _split_dot(m, t):
    hi = t.astype(BF16)
    lo = (t - hi.astype(F32)).astype(BF16)
    return _dot(m, hi) + _dot(m, lo)


def _rms(x, g):
    ms = jnp.mean(x * x, axis=-1, keepdims=True)
    return x * lax.rsqrt(ms + EPS) * g


def _sigmoid(x):
    return 1.0 / (1.0 + jnp.exp(-x))


def _softplus(x):
    return jnp.maximum(x, 0.0) + jnp.log1p(jnp.exp(-jnp.abs(x)))


TM_IN = 512
TN_IN = 640


def _in_proj_kernel(x_ref, g_ref, w_ref, o_ref, h_ref):
    @pl.when(pl.program_id(1) == 0)
    def _():
        h_ref[...] = _rms(x_ref[...], g_ref[...]).astype(BF16)

    o_ref[...] = _dot(h_ref[...], w_ref[...])


def _in_proj(x, g, w):
    n = x.shape[0]
    return pl.pallas_call(
        _in_proj_kernel,
        grid=(n // TM_IN, P_COLS // TN_IN),
        in_specs=[
            pl.BlockSpec((TM_IN, D_MODEL), lambda i, j: (i, 0)),
            pl.BlockSpec((1, D_MODEL), lambda i, j: (0, 0)),
            pl.BlockSpec((D_MODEL, TN_IN), lambda i, j: (0, j)),
        ],
        out_specs=pl.BlockSpec((TM_IN, TN_IN), lambda i, j: (i, j)),
        out_shape=jax.ShapeDtypeStruct((n, P_COLS), F32),
        scratch_shapes=[pltpu.VMEM((TM_IN, D_MODEL), BF16)],
        compiler_params=_params(("parallel", "arbitrary")),
        name="in_proj",
    )(x, g, w)


TM_PREP = 512


def _rope(t, c, s):
    lane = lax.broadcasted_iota(jnp.int32, t.shape, 1)
    first_half = (lane & (HEAD_DIM - 1)) < HEAD_DIM // 2
    swapped = jnp.where(first_half, pltpu.roll(t, GW - HEAD_DIM // 2, 1), pltpu.roll(t, HEAD_DIM // 2, 1))
    return t * c + swapped * s


def _prep_kernel(p_ref, c_ref, s_ref, o_ref):
    c = c_ref[...]
    s = s_ref[...]
    blk = lambda cb: p_ref[:, cb * GW:(cb + 1) * GW]
    o_ref[:, CB_QB * GW:(CB_QB + 1) * GW] = (blk(CB_QB) * QK_SCALE).astype(BF16)
    o_ref[:, CB_KB * GW:(CB_KB + 1) * GW] = blk(CB_KB).astype(BF16)
    o_ref[:, CB_VB * GW:(CB_VB + 1) * GW] = blk(CB_VB).astype(BF16)
    o_ref[:, CB_QD * GW:(CB_QD + 1) * GW] = (_rope(blk(CB_QD), c, s) * QK_SCALE).astype(BF16)
    o_ref[:, CB_KD * GW:(CB_KD + 1) * GW] = _rope(blk(CB_KD), c, s).astype(BF16)
    o_ref[:, CB_VD * GW:(CB_VD + 1) * GW] = blk(CB_VD).astype(BF16)


def _prep(p32, cos_t, sin_t, seq):
    n = p32.shape[0]
    nl = seq // TM_PREP
    return pl.pallas_call(
        _prep_kernel,
        grid=(n // TM_PREP,),
        in_specs=[
            pl.BlockSpec((TM_PREP, P16_COLS), lambda i: (i, 0)),
            pl.BlockSpec((TM_PREP, GW), lambda i: (i % nl, 0)),
            pl.BlockSpec((TM_PREP, GW), lambda i: (i % nl, 0)),
        ],
        out_specs=pl.BlockSpec((TM_PREP, P16_COLS), lambda i: (i, 0)),
        out_shape=jax.ShapeDtypeStruct((n, P16_COLS), BF16),
        compiler_params=_params(("parallel",)),
        name="prep",
    )(p32, cos_t, sin_t)


TM_OUT = 512


def _out_proj_kernel(ya_ref, yb_ref, yc_ref, o1_ref, l1_ref, o2_ref, l2_ref, o3_ref, l3_ref,
                     w_ref, x_ref, g_ref, o_ref):
    l1, l2, l3 = l1_ref[...], l2_ref[...], l3_ref[...]
    m = jnp.maximum(jnp.maximum(l1, l2), l3)
    e1, e2, e3 = jnp.exp(l1 - m), jnp.exp(l2 - m), jnp.exp(l3 - m)
    yd = (e1 * o1_ref[...] + e2 * o2_ref[...] + e3 * o3_ref[...]) / (e1 + e2 + e3)
    y = _dot(ya_ref[...].astype(BF16), w_ref[0 * GW:1 * GW, :])
    y += _dot(yb_ref[...].astype(BF16), w_ref[1 * GW:2 * GW, :])
    y += _dot(yc_ref[...].astype(BF16), w_ref[2 * GW:3 * GW, :])
    y += _dot(yd.astype(BF16), w_ref[3 * GW:4 * GW, :])
    o_ref[...] = x_ref[...] + _rms(y, g_ref[...])


def _out_proj(ya, yb, yc, dil, w, x, g):
    n = x.shape[0]
    row = pl.BlockSpec((TM_OUT, GW), lambda i: (i, 0))
    return pl.pallas_call(
        _out_proj_kernel,
        grid=(n // TM_OUT,),
        in_specs=[row] * 9 + [
            pl.BlockSpec((D_MODEL, D_MODEL), lambda i: (0, 0)),
            pl.BlockSpec((TM_OUT, D_MODEL), lambda i: (i, 0)),
            pl.BlockSpec((1, D_MODEL), lambda i: (0, 0)),
        ],
        out_specs=pl.BlockSpec((TM_OUT, D_MODEL), lambda i: (i, 0)),
        out_shape=jax.ShapeDtypeStruct((n, D_MODEL), F32),
        compiler_params=_params(("parallel",)),
        name="out_proj",
    )(ya, yb, yc, *dil, w, x, g)


TM_FFN = 512
TF_FFN = 1408


def _ffn_kernel(x_ref, g1_ref, wg_ref, wu_ref, wo_ref, g2_ref, o_ref, h_ref, acc_ref):
    j = pl.program_id(1)

    @pl.when(j == 0)
    def _():
        h_ref[...] = _rms(x_ref[...], g1_ref[...]).astype(BF16)
        acc_ref[...] = jnp.zeros_like(acc_ref)

    h = h_ref[...]
    gate = _dot(h, wg_ref[...])
    up = _dot(h, wu_ref[...])
    act = (gate * _sigmoid(gate) * up).astype(BF16)
    acc_ref[...] += _dot(act, wo_ref[...])

    @pl.when(j == pl.num_programs(1) - 1)
    def _():
        o_ref[...] = x_ref[...] + _rms(acc_ref[...], g2_ref[...])


def _ffn(x, g1, w_in, w_out, g2):
    n = x.shape[0]
    nf = D_FF // TF_FFN
    return pl.pallas_call(
        _ffn_kernel,
        grid=(n // TM_FFN, nf),
        in_specs=[
            pl.BlockSpec((TM_FFN, D_MODEL), lambda i, j: (i, 0)),
            pl.BlockSpec((1, D_MODEL), lambda i, j: (0, 0)),
            pl.BlockSpec((D_MODEL, TF_FFN), lambda i, j: (0, j)),
            pl.BlockSpec((D_MODEL, TF_FFN), lambda i, j: (0, nf + j)),
            pl.BlockSpec((TF_FFN, D_MODEL), lambda i, j: (j, 0)),
            pl.BlockSpec((1, D_MODEL), lambda i, j: (0, 0)),
        ],
        out_specs=pl.BlockSpec((TM_FFN, D_MODEL), lambda i, j: (i, 0)),
        out_shape=jax.ShapeDtypeStruct((n, D_MODEL), F32),
        scratch_shapes=[pltpu.VMEM((TM_FFN, D_MODEL), BF16), pltpu.VMEM((TM_FFN, D_MODEL), F32)],
        compiler_params=_params(("parallel", "arbitrary")),
        name="ffn",
    )(x, g1, w_in, w_in, w_out, g2)


TB_GLA = 512


def _gla_kernel(*refs, reverse, finalize):
    if finalize:
        q_ref, k_ref, v_ref, z_ref, wg_ref, bg_ref, of_ref, g_ref, ng_ref, o_ref, st_ref = refs
    else:
        q_ref, k_ref, v_ref, z_ref, wg_ref, bg_ref, o_ref, st_ref = refs
    C = GLA_CHUNK

    @pl.when(pl.program_id(1) == 0)
    def _():
        st_ref[...] = jnp.zeros_like(st_ref)

    logit = _dot(z_ref[...].astype(BF16), wg_ref[...]) + bg_ref[...]
    log_a = -_softplus(-logit) * (1.0 / GLA_TAU)

    r_i = lax.broadcasted_iota(jnp.int32, (C, C), 0)
    c_i = lax.broadcasted_iota(jnp.int32, (C, C), 1)
    cum = jnp.where((c_i >= r_i) if reverse else (c_i <= r_i), 1.0, 0.0).astype(BF16)
    r4 = lax.broadcasted_iota(jnp.int32, (HEADS * C, C), 0) & (C - 1)
    c4 = lax.broadcasted_iota(jnp.int32, (HEADS * C, C), 1)
    causal = (c4 >= r4) if reverse else (c4 <= r4)
    rb = lax.broadcasted_iota(jnp.int32, (GW, GW), 0) >> 6
    cb = lax.broadcasted_iota(jnp.int32, (GW, GW), 1) >> 6
    same_head = rb == cb
    last, mid = (0, C // 2) if reverse else (C - 1, C // 2 - 1)

    n_chunks = TB_GLA // C
    order = range(n_chunks - 1, -1, -1) if reverse else range(n_chunks)
    for c in order:
        rows = slice(c * C, (c + 1) * C)
        b = _split_dot(cum, log_a[rows])
        b_last = b[last:last + 1]
        b_mid = b[mid:mid + 1]
        q = q_ref[rows, :] * QK_SCALE
        k = k_ref[rows, :]
        v = v_ref[rows, :].astype(BF16)
        st = st_ref[...]
        att = _dot_nt(_stack_heads((q * jnp.exp(b - b_mid)).astype(BF16)),
                      (k * jnp.exp(b_mid - b)).astype(BF16))
        att = jnp.where(causal, att, 0.0)
        o = _unstack_heads(_dot(att.astype(BF16), v))
        o = o + _dot_nt((q * jnp.exp(b)).astype(BF16), st.astype(BF16))
        upd = _dot_tn(v, (k * jnp.exp(b_last - b)).astype(BF16))
        st_ref[...] = st * jnp.exp(b_last) + jnp.where(same_head, upd, 0.0)
        if finalize:
            o = o + of_ref[rows, :]
            blk = jnp.where(same_head, 1.0 / HEAD_DIM, 0.0).astype(BF16)
            ms = _split_dot_rhs(o * o, blk)
            g = g_ref[rows, :]
            o = o * lax.rsqrt(ms + EPS) * ng_ref[...] * (g * _sigmoid(g))
        o_ref[rows, :] = o


def _split_dot_rhs(t, m):
    hi = t.astype(BF16)
    lo = (t - hi.astype(F32)).astype(BF16)
    return _dot(hi, m) + _dot(lo, m)


def _gla_dir(p32, wg, bg, batch, seq, reverse, fin=None):
    nb = seq // TB_GLA
    n = batch * seq

    def rb(b, i):
        return b * nb + ((nb - 1 - i) if reverse else i)

    col = lambda cb: pl.BlockSpec((TB_GLA, GW), lambda b, i: (rb(b, i), cb))
    const = lambda shape: pl.BlockSpec(shape, lambda b, i: (0, 0))
    in_specs = [col(CB_QA), col(CB_KA), col(CB_VA),
                pl.BlockSpec((TB_GLA, Z_PAD), lambda b, i: (rb(b, i), Z_COL0 // Z_PAD)),
                const((Z_PAD, GW)), const((1, GW))]
    args = [p32, p32, p32, p32, wg, bg]
    if fin is not None:
        o_f, norm_g = fin
        in_specs += [pl.BlockSpec((TB_GLA, GW), lambda b, i: (rb(b, i), 0)), col(CB_GA), const((1, GW))]
        args += [o_f, p32, norm_g]
    return pl.pallas_call(
        functools.partial(_gla_kernel, reverse=reverse, finalize=fin is not None),
        grid=(batch, nb),
        in_specs=in_specs,
        out_specs=pl.BlockSpec((TB_GLA, GW), lambda b, i: (rb(b, i), 0)),
        out_shape=jax.ShapeDtypeStruct((n, GW), F32),
        scratch_shapes=[pltpu.VMEM((GW, GW), F32)],
        compiler_params=_params(("arbitrary", "arbitrary")),
        name="gla_bwd" if reverse else "gla_fwd",
    )(*args)


NA_BLOCK_ROWS = 8


def _nbr_kernel(q_ref, k_ref, v_ref, bias_ref, o_ref, *, grid_rows):
    W = GRID_W
    win = NA_ROWS * W
    r_i = lax.broadcasted_iota(jnp.int32, (HEADS * W, win), 0) & (W - 1)
    c_i = lax.broadcasted_iota(jnp.int32, (HEADS * W, win), 1) & (W - 1)
    col_start = jnp.clip(r_i - NA_COLS // 2, 0, W - NA_COLS)
    col_ok = (c_i >= col_start) & (c_i < col_start + NA_COLS)
    row0 = pl.program_id(1) * NA_BLOCK_ROWS

    def body(j, carry):
        r = row0 + j
        start = jnp.clip(r - NA_ROWS // 2, 0, grid_rows - NA_ROWS)
        pat = start - r + (NA_ROWS - 1)
        k0 = pl.multiple_of(start * W, W)
        q0 = pl.multiple_of(j * W, W)
        s = _dot_nt(_stack_heads(q_ref[pl.ds(q0, W), :]), k_ref[pl.ds(k0, win), :])
        s = jnp.where(col_ok, s + bias_ref[pat], -jnp.inf)
        m = jnp.max(s, axis=-1, keepdims=True)
        e = jnp.exp(s - m)
        p = e / jnp.sum(e, axis=-1, keepdims=True)
        o_ref[pl.ds(q0, W), :] = _unstack_heads(_dot(p.astype(BF16), v_ref[pl.ds(k0, win), :]))
        return carry

    lax.fori_loop(0, NA_BLOCK_ROWS, body, 0)


def _nbr(p16, bias, batch, seq):
    grid_rows = seq // GRID_W
    tq = NA_BLOCK_ROWS * GRID_W
    nb = seq // tq
    return pl.pallas_call(
        functools.partial(_nbr_kernel, grid_rows=grid_rows),
        grid=(batch, nb),
        in_specs=[
            pl.BlockSpec((tq, GW), lambda b, i: (b * nb + i, CB_QB)),
            pl.BlockSpec((seq, GW), lambda b, i: (b, CB_KB)),
            pl.BlockSpec((seq, GW), lambda b, i: (b, CB_VB)),
            pl.BlockSpec((NA_ROWS, HEADS * GRID_W, NA_ROWS * GRID_W), lambda b, i: (0, 0, 0)),
        ],
        out_specs=pl.BlockSpec((tq, GW), lambda b, i: (b * nb + i, 0)),
        out_shape=jax.ShapeDtypeStruct((batch * seq, GW), F32),
        compiler_params=_params(("parallel", "arbitrary")),
        name="nbr_attn",
    )(p16, p16, p16, bias)


def _nbr_bias(rpb):
    c = np.arange(GRID_W)
    dc = np.clip(c[None, :] - c[:, None], -(NA_COLS - 1), NA_COLS - 1) + NA_COLS - 1
    toeplitz = rpb[:, :, dc]
    pats = []
    for pat in range(NA_ROWS):
        t = toeplitz[:, pat:pat + NA_ROWS]
        pats.append(t.transpose(0, 2, 1, 3).reshape(HEADS * GRID_W, NA_ROWS * GRID_W))
    return jnp.stack(pats, axis=0).astype(F32)


TB_LRU = 512
SUB = 8


def _gelu_tanh(x):
    return 0.5 * x * (1.0 + jnp.tanh(0.7978845608028654 * (x + 0.044715 * x * x * x)))


def _lru_kernel(*refs, reverse, finalize):
    if finalize:
        (x_ref, xp_ref, xn_ref, cw_ref, cb_ref, wa_ref, ba_ref, wx_ref, bx_ref, lam_ref,
         hf_ref, gate_ref, o_ref, ext_ref, a_ref, u_ref, carry_ref) = refs
    else:
        (x_ref, xp_ref, xn_ref, cw_ref, cb_ref, wa_ref, ba_ref, wx_ref, bx_ref, lam_ref,
         o_ref, ext_ref, a_ref, u_ref, carry_ref) = refs
    TB = TB_LRU
    i = pl.program_id(1)
    nb = pl.num_programs(1)
    seq_blk = (nb - 1 - i) if reverse else i

    @pl.when(i == 0)
    def _():
        carry_ref[...] = jnp.zeros_like(carry_ref)

    ext_ref[0:SUB, :] = jnp.where(seq_blk == 0, 0.0, xp_ref[...])
    ext_ref[SUB:SUB + TB, :] = x_ref[...]
    ext_ref[SUB + TB:, :] = jnp.where(seq_blk == nb - 1, 0.0, xn_ref[...])
    xc = cb_ref[...] + jnp.zeros((TB, GW), F32)
    for j in range(4):
        xc = xc + ext_ref[pl.ds(SUB - 2 + j, TB), :] * cw_ref[j:j + 1, :]

    xcb = xc.astype(BF16)
    r = _sigmoid(_dot(xcb, wa_ref[...]) + ba_ref[...])
    gi = _sigmoid(_dot(xcb, wx_ref[...]) + bx_ref[...])
    log_a = (-LRU_C) * r * _softplus(-lam_ref[...])
    a = jnp.exp(log_a)
    u = jnp.sqrt(-jnp.tanh(log_a) * (a * a + 1.0)) * (gi * xc)

    rm = lax.broadcasted_iota(jnp.int32, (TB, GW), 0) & (SUB - 1)
    for d in (1, 2, 4):
        if reverse:
            a_sh, u_sh, ok = pltpu.roll(a, TB - d, 0), pltpu.roll(u, TB - d, 0), rm < SUB - d
        else:
            a_sh, u_sh, ok = pltpu.roll(a, d, 0), pltpu.roll(u, d, 0), rm >= d
        u = jnp.where(ok, a * u_sh + u, u)
        a = jnp.where(ok, a * a_sh, a)
    a_ref[...] = a
    u_ref[...] = u

    nt = TB // SUB
    edge = 0 if reverse else SUB - 1

    def body(t, carry):
        off = pl.multiple_of(((nt - 1 - t) if reverse else t) * SUB, SUB)
        h = u_ref[pl.ds(off, SUB), :] + a_ref[pl.ds(off, SUB), :] * carry
        u_ref[pl.ds(off, SUB), :] = h
        return jnp.broadcast_to(h[edge:edge + 1, :], (SUB, GW))

    carry_ref[...] = lax.fori_loop(0, nt, body, carry_ref[...])

    h = u_ref[...]
    if finalize:
        h = (h + hf_ref[...]) * _gelu_tanh(gate_ref[...])
    o_ref[...] = h


def _lru_dir(p32, w, batch, seq, reverse, fin=None):
    nb = seq // TB_LRU
    n = batch * seq
    tiles = TB_LRU // SUB

    def rb(b, i):
        return b * nb + ((nb - 1 - i) if reverse else i)

    const = lambda shape: pl.BlockSpec(shape, lambda b, i: (0, 0))
    in_specs = [
        pl.BlockSpec((TB_LRU, GW), lambda b, i: (rb(b, i), CB_XC)),
        pl.BlockSpec((SUB, GW), lambda b, i: (jnp.maximum(rb(b, i) * tiles - 1, 0), CB_XC)),
        pl.BlockSpec((SUB, GW), lambda b, i: (jnp.minimum((rb(b, i) + 1) * tiles, n // SUB - 1), CB_XC)),
        const((4, GW)), const((1, GW)), const((GW, GW)), const((1, GW)), const((GW, GW)), const((1, GW)),
        const((1, GW)),
    ]
    args = [p32, p32, p32, w["conv_w"], w["conv_b"], w["wa"], w["ba"], w["wx"], w["bx"], w["lam"]]
    if fin is not None:
        in_specs += [pl.BlockSpec((TB_LRU, GW), lambda b, i: (rb(b, i), 0)),
                     pl.BlockSpec((TB_LRU, GW), lambda b, i: (rb(b, i), CB_GC))]
        args += [fin, p32]
    return pl.pallas_call(
        functools.partial(_lru_kernel, reverse=reverse, finalize=fin is not None),
        grid=(batch, nb),
        in_specs=in_specs,
        out_specs=pl.BlockSpec((TB_LRU, GW), lambda b, i: (rb(b, i), 0)),
        out_shape=jax.ShapeDtypeStruct((n, GW), F32),
        scratch_shapes=[pltpu.VMEM((TB_LRU + 2 * SUB, GW), F32), pltpu.VMEM((TB_LRU, GW), F32),
                        pltpu.VMEM((TB_LRU, GW), F32), pltpu.VMEM((SUB, GW), F32)],
        compiler_params=_params(("arbitrary", "arbitrary")),
        name="lru_bwd" if reverse else "lru_fwd",
    )(*args)


TJ_DIL = 128
WIN_DIL = TJ_DIL + 2 * DIL_RADIUS


def _dil_kernel(q_ref, k_ref, v_ref, o_ref, l_ref, *, sub_len):
    j0 = pl.program_id(2) * TJ_DIL
    ws = pl.multiple_of(jnp.clip(j0 - DIL_RADIUS, 0, sub_len - WIN_DIL), DIL_RADIUS)
    s = _dot_nt(_stack_heads(q_ref[...]), k_ref[pl.ds(ws, WIN_DIL), :])
    qpos = j0 + (lax.broadcasted_iota(jnp.int32, s.shape, 0) & (TJ_DIL - 1))
    kpos = ws + lax.broadcasted_iota(jnp.int32, s.shape, 1)
    s = jnp.where(jnp.abs(kpos - qpos) <= DIL_RADIUS, s, -jnp.inf)
    m = jnp.max(s, axis=-1, keepdims=True)
    e = jnp.exp(s - m)
    den = jnp.sum(e, axis=-1, keepdims=True)
    o_ref[...] = _unstack_heads(_dot((e / den).astype(BF16), v_ref[pl.ds(ws, WIN_DIL), :]))
    l_ref[...] = _unstack_heads(jnp.broadcast_to(m + jnp.log(den), (HEADS * TJ_DIL, GW)))


def _dil_branch(p16, batch, seq, dil):
    sub_len = seq // dil
    pv = p16.reshape(batch, sub_len, dil * P16_COLS)
    cpt = P16_COLS // GW
    out_sd = jax.ShapeDtypeStruct((batch, sub_len, dil * GW), F32)
    o, lse = pl.pallas_call(
        functools.partial(_dil_kernel, sub_len=sub_len),
        grid=(batch, dil, sub_len // TJ_DIL),
        in_specs=[
            pl.BlockSpec((None, TJ_DIL, GW), lambda b, r, j: (b, j, r * cpt + CB_QD)),
            pl.BlockSpec((None, sub_len, GW), lambda b, r, j: (b, 0, r * cpt + CB_KD)),
            pl.BlockSpec((None, sub_len, GW), lambda b, r, j: (b, 0, r * cpt + CB_VD)),
        ],
        out_specs=[pl.BlockSpec((None, TJ_DIL, GW), lambda b, r, j: (b, j, r))] * 2,
        out_shape=[out_sd, out_sd],
        compiler_params=_params(("parallel", "parallel", "arbitrary")),
        name=f"dil_attn_d{dil}",
    )(pv, pv, pv)
    n = batch * seq
    return o.reshape(n, GW), lse.reshape(n, GW)


def _reorder_w_in(w_in):
    qa = 0
    za = 4 * GW
    qb = za + 2 * GLA_RANK
    xc = qb + 3 * GW
    qd = xc + 2 * GW
    end = qd + 3 * GW
    pad = jnp.zeros(w_in.shape[:-1] + (Z_PAD - 2 * GLA_RANK,), w_in.dtype)
    return jnp.concatenate([w_in[..., qb:xc], w_in[..., qd:end], w_in[..., qa:za], w_in[..., xc:qd],
                            w_in[..., za:qb], pad], axis=-1).astype(BF16)


def _block_diag(w):
    out = jnp.zeros((GW, GW), w.dtype)
    for h in range(HEADS):
        out = out.at[h * HEAD_DIM:(h + 1) * HEAD_DIM, h * HEAD_DIM:(h + 1) * HEAD_DIM].set(w[h])
    return out


def _gate_weight(w_gate, direction):
    out = jnp.zeros((Z_PAD, GW), F32)
    return out.at[direction * GLA_RANK:(direction + 1) * GLA_RANK].set(w_gate[direction]).astype(BF16)


def _rope_tables(seq):
    pos = jnp.arange(seq, dtype=F32)
    inv_freq = ROPE_THETA ** (-jnp.arange(0, HEAD_DIM, 2, dtype=F32) / HEAD_DIM)
    ang = pos[:, None] * inv_freq[None, :]
    cos, sin = jnp.cos(ang), jnp.sin(ang)
    return jnp.tile(jnp.concatenate([cos, cos], -1), (1, HEADS)), jnp.tile(jnp.concatenate([-sin, sin], -1), (1, HEADS))


def kernel(x, mix_norm_pre, mix_norm_post, w_in, gla_w_gate, gla_b_gate, gla_norm, na_rpb, lru_conv_w, lru_conv_b, lru_w_a, lru_b_a, lru_w_x, lru_b_x, lru_lambda, w_out, ffn_norm_pre, ffn_norm_post, ffn_w_in, ffn_w_out):
    batch, seq, d_model = x.shape
    assert d_model == D_MODEL and seq % (16 * WIN_DIL) == 0 and seq % TB_GLA == 0
    depth = w_in.shape[0]
    n = batch * seq
    xf = x.reshape(n, D_MODEL)
    cos_t, sin_t = _rope_tables(seq)
    w_in_r = _reorder_w_in(w_in)
    w_out_b = w_out.astype(BF16)
    ffn_w_in_b = ffn_w_in.astype(BF16)
    ffn_w_out_b = ffn_w_out.astype(BF16)
    row = lambda t: t.reshape(1, -1)

    for l in range(depth):
        p32 = _in_proj(xf, row(mix_norm_pre[l]), w_in_r[l])
        p16 = _prep(p32, cos_t, sin_t, seq)

        o_f = _gla_dir(p32, _gate_weight(gla_w_gate[l], 0), row(gla_b_gate[l, 0]), batch, seq, False)
        ya = _gla_dir(p32, _gate_weight(gla_w_gate[l], 1), row(gla_b_gate[l, 1]), batch, seq, True,
                      fin=(o_f, row(gla_norm[l])))

        yb = _nbr(p16, _nbr_bias(na_rpb[l]), batch, seq)

        lru_w = lambda e: dict(conv_w=lru_conv_w[l], conv_b=row(lru_conv_b[l]),
                               wa=_block_diag(lru_w_a[l, e]).astype(BF16), ba=row(lru_b_a[l, e]),
                               wx=_block_diag(lru_w_x[l, e]).astype(BF16), bx=row(lru_b_x[l, e]),
                               lam=row(lru_lambda[l, e]))
        h_f = _lru_dir(p32, lru_w(0), batch, seq, False)
        yc = _lru_dir(p32, lru_w(1), batch, seq, True, fin=h_f)

        dil = []
        for _, d in DIL_PAIRS:
            dil.extend(_dil_branch(p16, batch, seq, d))

        xf = _out_proj(ya, yb, yc, dil, w_out_b[l], xf, row(mix_norm_post[l]))
        xf = _ffn(xf, row(ffn_norm_pre[l]), ffn_w_in_b[l], ffn_w_out_b[l], row(ffn_norm_post[l]))
    return xf.reshape(batch, seq, D_MODEL)
```

```python
import functools

import numpy as np
import jax
import jax.numpy as jnp
from jax import lax
from jax.experimental import pallas as pl
from jax.experimental.pallas import tpu as pltpu

F32 = jnp.float32
BF16 = jnp.bfloat16

D_MODEL = 1024
HEAD_DIM = 64
HEADS = 4
GW = HEADS * HEAD_DIM
GLA_RANK = 16
GLA_TAU = 16.0
GLA_CHUNK = 64
GRID_W = 64
NA_ROWS = 8
NA_COLS = 16
LRU_C = 8.0
DILATIONS = (1, 4, 16)
DIL_RADIUS = 64
ROPE_THETA = 10000.0
D_FF = 2816
EPS = 1e-6
QK_SCALE = HEAD_DIM ** -0.5
LANES = 128

CB_QA, CB_KA, CB_VA, CB_GA, CB_XC, CB_GC = 0, 1, 2, 3, 4, 5
Z_COL0 = 6 * GW
PA_COLS = Z_COL0 + LANES
QKV = 3 * GW
P_COLS = PA_COLS + 2 * QKV

VMEM_LIMIT = 56 * 1024 * 1024


def _params(sem, vmem=VMEM_LIMIT):
    return pltpu.CompilerParams(dimension_semantics=sem, vmem_limit_bytes=vmem)


def _head_mask(rows, h):
    lane = lax.broadcasted_iota(jnp.int32, (rows, GW), 1)
    return (lane >> 6) == h


def _stack_heads(t):
    rows = t.shape[0]
    zero = jnp.zeros_like(t)
    return jnp.concatenate([jnp.where(_head_mask(rows, h), t, zero) for h in range(HEADS)], axis=0)


def _unstack_heads(t):
    rows = t.shape[0] // HEADS
    out = jnp.where(_head_mask(rows, 0), t[0:rows], 0.0)
    for h in range(1, HEADS):
        out = out + jnp.where(_head_mask(rows, h), t[h * rows:(h + 1) * rows], 0.0)
    return out


def _dot(a, b):
    return jnp.dot(a, b, preferred_element_type=F32)


def _dot_nt(a, b):
    return lax.dot_general(a, b, (((1,), (1,)), ((), ())), preferred_element_type=F32)


def _dot_tn(a, b):
    return lax.dot_general(a, b, (((0,), (0,)), ((), ())), preferred_element_type=F32)


def _split(t):
    hi = t.astype(BF16)
    return hi, (t - hi.astype(F32)).astype(BF16)


def _split_dot(m, t):
    hi, lo = _split(t)
    return _dot(m, hi) + _dot(m, lo)


def _split_dot_rhs(t, m):
    hi, lo = _split(t)
    return _dot(hi, m) + _dot(lo, m)


def _rms(x, g):
    ms = jnp.mean(x * x, axis=-1, keepdims=True)
    return x * lax.rsqrt(ms + EPS) * g


def _sigmoid(x):
    return 1.0 / (1.0 + jnp.exp(-x))


def _softplus(x):
    return jnp.maximum(x, 0.0) + jnp.log1p(jnp.exp(-jnp.abs(x)))


def _softmax_pv(s, v):
    m = jnp.max(s, axis=-1, keepdims=True)
    e = jnp.exp(s - m).astype(BF16)
    pv = _dot(e, jnp.concatenate([v, jnp.ones((v.shape[0], LANES), BF16)], axis=1))
    den = _unstack_heads(jnp.concatenate([pv[:, GW:], pv[:, GW:]], axis=1))
    o = _unstack_heads(pv[:, :GW]) / den
    lse = _unstack_heads(jnp.broadcast_to(m, (s.shape[0], GW))) + jnp.log(den)
    return o, lse


TM_IN = 512


def _rope(t, c, s):
    lane = lax.broadcasted_iota(jnp.int32, t.shape, 1)
    first_half = (lane & (HEAD_DIM - 1)) < HEAD_DIM // 2
    swapped = jnp.where(first_half, pltpu.roll(t, GW - HEAD_DIM // 2, 1), pltpu.roll(t, HEAD_DIM // 2, 1))
    return t * c + swapped * s


def _in_proj_kernel(x_ref, g_ref, w_ref, c_ref, s_ref, pa_ref, pb_ref, d1_ref, d4_ref, d16_ref, rope_ref):
    h = _rms(x_ref[...], g_ref[...]).astype(BF16)
    pa_ref[...] = _dot(h, w_ref[:, 0:PA_COLS])
    pb = _dot(h, w_ref[:, PA_COLS:PA_COLS + QKV])
    pb_ref[:, 0:GW] = (pb[:, 0:GW] * QK_SCALE).astype(BF16)
    pb_ref[:, GW:QKV] = pb[:, GW:QKV].astype(BF16)
    pd = _dot(h, w_ref[:, PA_COLS + QKV:P_COLS])
    c = c_ref[...]
    s = s_ref[...]
    qkv = jnp.concatenate([_rope(pd[:, 0:GW], c, s) * QK_SCALE, _rope(pd[:, GW:2 * GW], c, s),
                           pd[:, 2 * GW:QKV]], axis=1)
    d1_ref[...] = qkv.astype(BF16)
    for t in range(QKV // LANES):
        lanes = slice(t * LANES, (t + 1) * LANES)
        rope_ref[t] = qkv[:, lanes]
        for d, ref in ((4, d4_ref), (16, d16_ref)):
            for r in range(d):
                ref[r, :, lanes] = rope_ref[t, pl.ds(r, TM_IN // d, stride=d), :].astype(BF16)


def _in_proj(x, g, w, cos_t, sin_t, batch, seq):
    n = x.shape[0]
    nl = seq // TM_IN
    row = lambda cols: pl.BlockSpec((TM_IN, cols), lambda i: (i, 0))
    const = lambda shape: pl.BlockSpec(shape, lambda i: (0, 0))
    tab = pl.BlockSpec((TM_IN, GW), lambda i: (i % nl, 0))
    res = lambda d: pl.BlockSpec((None, d, TM_IN // d, QKV), lambda i: (i // nl, 0, i % nl, 0))
    return pl.pallas_call(
        _in_proj_kernel,
        grid=(n // TM_IN,),
        in_specs=[row(D_MODEL), const((1, D_MODEL)), const((D_MODEL, P_COLS)), tab, tab],
        out_specs=[row(PA_COLS), row(QKV), row(QKV), res(4), res(16)],
        out_shape=[jax.ShapeDtypeStruct((n, PA_COLS), F32),
                   jax.ShapeDtypeStruct((n, QKV), BF16),
                   jax.ShapeDtypeStruct((n, QKV), BF16),
                   jax.ShapeDtypeStruct((batch, 4, seq // 4, QKV), BF16),
                   jax.ShapeDtypeStruct((batch, 16, seq // 16, QKV), BF16)],
        scratch_shapes=[pltpu.VMEM((QKV // LANES, TM_IN, LANES), F32)],
        compiler_params=_params(("parallel",)),
        name="in_proj",
    )(x, g, w, cos_t, sin_t)


TM_OUT = 512


def _out_proj_kernel(ya_ref, yb_ref, yc_ref, yd_ref, w_ref, x_ref, g_ref, o_ref):
    y = _dot(ya_ref[...].astype(BF16), w_ref[0 * GW:1 * GW, :])
    y += _dot(yb_ref[...].astype(BF16), w_ref[1 * GW:2 * GW, :])
    y += _dot(yc_ref[...].astype(BF16), w_ref[2 * GW:3 * GW, :])
    y += _dot(yd_ref[...].astype(BF16), w_ref[3 * GW:4 * GW, :])
    o_ref[...] = x_ref[...] + _rms(y, g_ref[...])


def _out_proj(ya, yb, yc, yd, w, x, g):
    n = x.shape[0]
    row = pl.BlockSpec((TM_OUT, GW), lambda i: (i, 0))
    return pl.pallas_call(
        _out_proj_kernel,
        grid=(n // TM_OUT,),
        in_specs=[row] * 4 + [
            pl.BlockSpec((D_MODEL, D_MODEL), lambda i: (0, 0)),
            pl.BlockSpec((TM_OUT, D_MODEL), lambda i: (i, 0)),
            pl.BlockSpec((1, D_MODEL), lambda i: (0, 0)),
        ],
        out_specs=pl.BlockSpec((TM_OUT, D_MODEL), lambda i: (i, 0)),
        out_shape=jax.ShapeDtypeStruct((n, D_MODEL), F32),
        compiler_params=_params(("parallel",)),
        name="out_proj",
    )(ya, yb, yc, yd, w, x, g)


TM_FFN = 512
TF_FFN = 1408


def _ffn_kernel(x_ref, g1_ref, wg_ref, wu_ref, wo_ref, g2_ref, o_ref, h_ref, acc_ref):
    j = pl.program_id(1)

    @pl.when(j == 0)
    def _():
        h_ref[...] = _rms(x_ref[...], g1_ref[...]).astype(BF16)
        acc_ref[...] = jnp.zeros_like(acc_ref)

    h = h_ref[...]
    gate = _dot(h, wg_ref[...])
    up = _dot(h, wu_ref[...])
    act = (gate * _sigmoid(gate) * up).astype(BF16)
    acc_ref[...] += _dot(act, wo_ref[...])

    @pl.when(j == pl.num_programs(1) - 1)
    def _():
        o_ref[...] = x_ref[...] + _rms(acc_ref[...], g2_ref[...])


def _ffn(x, g1, w_in, w_out, g2):
    n = x.shape[0]
    nf = D_FF // TF_FFN
    return pl.pallas_call(
        _ffn_kernel,
        grid=(n // TM_FFN, nf),
        in_specs=[
            pl.BlockSpec((TM_FFN, D_MODEL), lambda i, j: (i, 0)),
            pl.BlockSpec((1, D_MODEL), lambda i, j: (0, 0)),
            pl.BlockSpec((D_MODEL, TF_FFN), lambda i, j: (0, j)),
            pl.BlockSpec((D_MODEL, TF_FFN), lambda i, j: (0, nf + j)),
            pl.BlockSpec((TF_FFN, D_MODEL), lambda i, j: (j, 0)),
            pl.BlockSpec((1, D_MODEL), lambda i, j: (0, 0)),
        ],
        out_specs=pl.BlockSpec((TM_FFN, D_MODEL), lambda i, j: (i, 0)),
        out_shape=jax.ShapeDtypeStruct((n, D_MODEL), F32),
        scratch_shapes=[pltpu.VMEM((TM_FFN, D_MODEL), BF16), pltpu.VMEM((TM_FFN, D_MODEL), F32)],
        compiler_params=_params(("parallel", "arbitrary")),
        name="ffn",
    )(x, g1, w_in, w_in, w_out, g2)


TB_GLA = 512


def _gla_kernel(*refs, reverse, finalize):
    if finalize:
        q_ref, k_ref, v_ref, z_ref, wg_ref, bg_ref, of_ref, g_ref, ng_ref, o_ref, st_ref = refs
    else:
        q_ref, k_ref, v_ref, z_ref, wg_ref, bg_ref, o_ref, st_ref = refs
    C = GLA_CHUNK

    @pl.when(pl.program_id(1) == 0)
    def _():
        st_ref[...] = jnp.zeros_like(st_ref)

    logit = _dot(z_ref[...].astype(BF16), wg_ref[...]) + bg_ref[...]
    log_a = -_softplus(-logit) * (1.0 / GLA_TAU)

    r_i = lax.broadcasted_iota(jnp.int32, (C, C), 0)
    c_i = lax.broadcasted_iota(jnp.int32, (C, C), 1)
    cum = jnp.where((c_i >= r_i) if reverse else (c_i <= r_i), 1.0, 0.0).astype(BF16)
    r4 = lax.broadcasted_iota(jnp.int32, (HEADS * C, C), 0) & (C - 1)
    c4 = lax.broadcasted_iota(jnp.int32, (HEADS * C, C), 1)
    causal = (c4 >= r4) if reverse else (c4 <= r4)
    rb = lax.broadcasted_iota(jnp.int32, (GW, GW), 0) >> 6
    cb = lax.broadcasted_iota(jnp.int32, (GW, GW), 1) >> 6
    same_head = rb == cb
    last, mid = (0, C // 2) if reverse else (C - 1, C // 2 - 1)

    n_chunks = TB_GLA // C
    order = range(n_chunks - 1, -1, -1) if reverse else range(n_chunks)
    for c in order:
        rows = slice(c * C, (c + 1) * C)
        b = _split_dot(cum, log_a[rows])
        b_last = b[last:last + 1]
        b_mid = b[mid:mid + 1]
        q = q_ref[rows, :] * QK_SCALE
        k = k_ref[rows, :]
        v = v_ref[rows, :].astype(BF16)
        st = st_ref[...]
        att = _dot_nt(_stack_heads((q * jnp.exp(b - b_mid)).astype(BF16)),
                      (k * jnp.exp(b_mid - b)).astype(BF16))
        att = jnp.where(causal, att, 0.0)
        o = _unstack_heads(_dot(att.astype(BF16), v))
        o = o + _dot_nt((q * jnp.exp(b)).astype(BF16), st.astype(BF16))
        upd = _dot_tn(v, (k * jnp.exp(b_last - b)).astype(BF16))
        st_ref[...] = st * jnp.exp(b_last) + jnp.where(same_head, upd, 0.0)
        if finalize:
            o = o + of_ref[rows, :]
            blk = jnp.where(same_head, 1.0 / HEAD_DIM, 0.0).astype(BF16)
            ms = _split_dot_rhs(o * o, blk)
            g = g_ref[rows, :]
            o = o * lax.rsqrt(ms + EPS) * ng_ref[...] * (g * _sigmoid(g))
        o_ref[rows, :] = o


def _gla_dir(pa, wg, bg, batch, seq, reverse, fin=None):
    nb = seq // TB_GLA
    n = batch * seq

    def rb(b, i):
        return b * nb + ((nb - 1 - i) if reverse else i)

    col = lambda cb: pl.BlockSpec((TB_GLA, GW), lambda b, i: (rb(b, i), cb))
    const = lambda shape: pl.BlockSpec(shape, lambda b, i: (0, 0))
    in_specs = [col(CB_QA), col(CB_KA), col(CB_VA),
                pl.BlockSpec((TB_GLA, LANES), lambda b, i: (rb(b, i), Z_COL0 // LANES)),
                const((LANES, GW)), const((1, GW))]
    args = [pa, pa, pa, pa, wg, bg]
    if fin is not None:
        o_f, norm_g = fin
        in_specs += [pl.BlockSpec((TB_GLA, GW), lambda b, i: (rb(b, i), 0)), col(CB_GA), const((1, GW))]
        args += [o_f, pa, norm_g]
    return pl.pallas_call(
        functools.partial(_gla_kernel, reverse=reverse, finalize=fin is not None),
        grid=(batch, nb),
        in_specs=in_specs,
        out_specs=pl.BlockSpec((TB_GLA, GW), lambda b, i: (rb(b, i), 0)),
        out_shape=jax.ShapeDtypeStruct((n, GW), F32),
        scratch_shapes=[pltpu.VMEM((GW, GW), F32)],
        compiler_params=_params(("arbitrary", "arbitrary")),
        name="gla_bwd" if reverse else "gla_fwd",
    )(*args)


NA_BLOCK_ROWS = 8


def _nbr_kernel(q_ref, k_ref, v_ref, bias_ref, o_ref, *, grid_rows):
    W = GRID_W
    win = NA_ROWS * W
    r_i = lax.broadcasted_iota(jnp.int32, (HEADS * W, win), 0) & (W - 1)
    c_i = lax.broadcasted_iota(jnp.int32, (HEADS * W, win), 1) & (W - 1)
    col_start = jnp.clip(r_i - NA_COLS // 2, 0, W - NA_COLS)
    col_ok = (c_i >= col_start) & (c_i < col_start + NA_COLS)
    row0 = pl.program_id(1) * NA_BLOCK_ROWS

    def body(j, carry):
        r = row0 + j
        start = jnp.clip(r - NA_ROWS // 2, 0, grid_rows - NA_ROWS)
        pat = start - r + (NA_ROWS - 1)
        k0 = pl.multiple_of(start * W, W)
        q0 = pl.multiple_of(j * W, W)
        s = _dot_nt(_stack_heads(q_ref[pl.ds(q0, W), :]), k_ref[pl.ds(k0, win), :])
        s = jnp.where(col_ok, s + bias_ref[pat], -jnp.inf)
        o, _ = _softmax_pv(s, v_ref[pl.ds(k0, win), :])
        o_ref[pl.ds(q0, W), :] = o
        return carry

    lax.fori_loop(0, NA_BLOCK_ROWS, body, 0, unroll=2)


def _nbr(pb, bias, batch, seq):
    grid_rows = seq // GRID_W
    tq = NA_BLOCK_ROWS * GRID_W
    nb = seq // tq
    return pl.pallas_call(
        functools.partial(_nbr_kernel, grid_rows=grid_rows),
        grid=(batch, nb),
        in_specs=[
            pl.BlockSpec((tq, GW), lambda b, i: (b * nb + i, 0)),
            pl.BlockSpec((seq, GW), lambda b, i: (b, 1)),
            pl.BlockSpec((seq, GW), lambda b, i: (b, 2)),
            pl.BlockSpec((NA_ROWS, HEADS * GRID_W, NA_ROWS * GRID_W), lambda b, i: (0, 0, 0)),
        ],
        out_specs=pl.BlockSpec((tq, GW), lambda b, i: (b * nb + i, 0)),
        out_shape=jax.ShapeDtypeStruct((batch * seq, GW), F32),
        compiler_params=_params(("parallel", "arbitrary")),
        name="nbr_attn",
    )(pb, pb, pb, bias)


def _nbr_bias(rpb):
    c = np.arange(GRID_W)
    dc = np.clip(c[None, :] - c[:, None], -(NA_COLS - 1), NA_COLS - 1) + NA_COLS - 1
    onehot = jnp.asarray(dc.reshape(-1)[None, :] == np.arange(2 * NA_COLS - 1)[:, None], F32)
    toeplitz = jnp.dot(rpb.reshape(-1, 2 * NA_COLS - 1), onehot, precision=lax.Precision.HIGHEST)
    toeplitz = toeplitz.reshape(HEADS, 2 * NA_ROWS - 1, GRID_W, GRID_W)
    pats = []
    for pat in range(NA_ROWS):
        t = toeplitz[:, pat:pat + NA_ROWS]
        pats.append(t.transpose(0, 2, 1, 3).reshape(HEADS * GRID_W, NA_ROWS * GRID_W))
    return jnp.stack(pats, axis=0)


TB_LRU = 512
SUB = 8


def _gelu_tanh(x):
    return 0.5 * x * (1.0 + jnp.tanh(0.7978845608028654 * (x + 0.044715 * x * x * x)))


def _lru_kernel(*refs, reverse, finalize):
    if finalize:
        (x_ref, xp_ref, xn_ref, cw_ref, cb_ref, wa_ref, ba_ref, wx_ref, bx_ref, lam_ref,
         hf_ref, gate_ref, o_ref, ext_ref, a_ref, u_ref, carry_ref) = refs
    else:
        (x_ref, xp_ref, xn_ref, cw_ref, cb_ref, wa_ref, ba_ref, wx_ref, bx_ref, lam_ref,
         o_ref, ext_ref, a_ref, u_ref, carry_ref) = refs
    TB = TB_LRU
    i = pl.program_id(1)
    nb = pl.num_programs(1)
    seq_blk = (nb - 1 - i) if reverse else i

    @pl.when(i == 0)
    def _():
        carry_ref[...] = jnp.zeros_like(carry_ref)

    ext_ref[0:SUB, :] = jnp.where(seq_blk == 0, 0.0, xp_ref[...])
    ext_ref[SUB:SUB + TB, :] = x_ref[...]
    ext_ref[SUB + TB:, :] = jnp.where(seq_blk == nb - 1, 0.0, xn_ref[...])
    xc = cb_ref[...] + jnp.zeros((TB, GW), F32)
    for j in range(4):
        xc = xc + ext_ref[pl.ds(SUB - 2 + j, TB), :] * cw_ref[j:j + 1, :]

    xcb = xc.astype(BF16)
    r = _sigmoid(_dot(xcb, wa_ref[...]) + ba_ref[...])
    gi = _sigmoid(_dot(xcb, wx_ref[...]) + bx_ref[...])
    log_a = (-LRU_C) * r * _softplus(-lam_ref[...])
    a = jnp.exp(log_a)
    u = jnp.sqrt(-jnp.tanh(log_a) * (a * a + 1.0)) * (gi * xc)

    rm = lax.broadcasted_iota(jnp.int32, (TB, GW), 0) & (SUB - 1)
    for d in (1, 2, 4):
        if reverse:
            a_sh, u_sh, ok = pltpu.roll(a, TB - d, 0), pltpu.roll(u, TB - d, 0), rm < SUB - d
        else:
            a_sh, u_sh, ok = pltpu.roll(a, d, 0), pltpu.roll(u, d, 0), rm >= d
        u = jnp.where(ok, a * u_sh + u, u)
        a = jnp.where(ok, a * a_sh, a)
    a_ref[...] = a
    u_ref[...] = u

    nt = TB // SUB
    edge = 0 if reverse else SUB - 1

    def body(t, carry):
        off = pl.multiple_of(((nt - 1 - t) if reverse else t) * SUB, SUB)
        h = u_ref[pl.ds(off, SUB), :] + a_ref[pl.ds(off, SUB), :] * carry
        u_ref[pl.ds(off, SUB), :] = h
        return jnp.broadcast_to(h[edge:edge + 1, :], (SUB, GW))

    carry_ref[...] = lax.fori_loop(0, nt, body, carry_ref[...])

    h = u_ref[...]
    if finalize:
        h = (h + hf_ref[...]) * _gelu_tanh(gate_ref[...])
    o_ref[...] = h


def _lru_dir(pa, w, batch, seq, reverse, fin=None):
    nb = seq // TB_LRU
    n = batch * seq
    tiles = TB_LRU // SUB

    def rb(b, i):
        return b * nb + ((nb - 1 - i) if reverse else i)

    const = lambda shape: pl.BlockSpec(shape, lambda b, i: (0, 0))
    in_specs = [
        pl.BlockSpec((TB_LRU, GW), lambda b, i: (rb(b, i), CB_XC)),
        pl.BlockSpec((SUB, GW), lambda b, i: (jnp.maximum(rb(b, i) * tiles - 1, 0), CB_XC)),
        pl.BlockSpec((SUB, GW), lambda b, i: (jnp.minimum((rb(b, i) + 1) * tiles, n // SUB - 1), CB_XC)),
        const((4, GW)), const((1, GW)), const((GW, GW)), const((1, GW)), const((GW, GW)), const((1, GW)),
        const((1, GW)),
    ]
    args = [pa, pa, pa, w["conv_w"], w["conv_b"], w["wa"], w["ba"], w["wx"], w["bx"], w["lam"]]
    if fin is not None:
        in_specs += [pl.BlockSpec((TB_LRU, GW), lambda b, i: (rb(b, i), 0)),
                     pl.BlockSpec((TB_LRU, GW), lambda b, i: (rb(b, i), CB_GC))]
        args += [fin, pa]
    return pl.pallas_call(
        functools.partial(_lru_kernel, reverse=reverse, finalize=fin is not None),
        grid=(batch, nb),
        in_specs=in_specs,
        out_specs=pl.BlockSpec((TB_LRU, GW), lambda b, i: (rb(b, i), 0)),
        out_shape=jax.ShapeDtypeStruct((n, GW), F32),
        scratch_shapes=[pltpu.VMEM((TB_LRU + 2 * SUB, GW), F32), pltpu.VMEM((TB_LRU, GW), F32),
                        pltpu.VMEM((TB_LRU, GW), F32), pltpu.VMEM((SUB, GW), F32)],
        compiler_params=_params(("arbitrary", "arbitrary")),
        name="lru_bwd" if reverse else "lru_fwd",
    )(*args)


TJ_DIL = 512
SB_DIL = 128
WIN_DIL = SB_DIL + 2 * DIL_RADIUS


def _dil_kernel(*refs, sub_len, merge):
    if merge:
        (q_ref, k_ref, v_ref, o4_ref, l4_ref, o16_ref, l16_ref, o_ref,
         n4o_ref, n4l_ref, n16o_ref, n16l_ref) = refs
        for d, src_o, src_l, dst_o, dst_l in ((4, o4_ref, l4_ref, n4o_ref, n4l_ref),
                                              (16, o16_ref, l16_ref, n16o_ref, n16l_ref)):
            for r in range(d):
                for t in range(GW // LANES):
                    lanes = slice(t * LANES, (t + 1) * LANES)
                    dst_o[t, pl.ds(r, TJ_DIL // d, stride=d), :] = src_o[r, :, lanes]
                    dst_l[t, pl.ds(r, TJ_DIL // d, stride=d), :] = src_l[r, :, lanes]
    else:
        q_ref, k_ref, v_ref, o_ref, l_ref = refs
    j_blk = pl.program_id(2) * TJ_DIL
    shape = (HEADS * SB_DIL, WIN_DIL)
    delta = lax.broadcasted_iota(jnp.int32, shape, 1) - (lax.broadcasted_iota(jnp.int32, shape, 0) & (SB_DIL - 1))
    for sb in range(TJ_DIL // SB_DIL):
        rows = slice(sb * SB_DIL, (sb + 1) * SB_DIL)
        j0 = j_blk + sb * SB_DIL
        ws = pl.multiple_of(jnp.clip(j0 - DIL_RADIUS, 0, sub_len - WIN_DIL), DIL_RADIUS)
        s = _dot_nt(_stack_heads(q_ref[rows, :]), k_ref[pl.ds(ws, WIN_DIL), :])
        s = jnp.where(jnp.abs(delta + (ws - j0)) <= DIL_RADIUS, s, -jnp.inf)
        o, lse = _softmax_pv(s, v_ref[pl.ds(ws, WIN_DIL), :])
        if merge:
            tok = lambda ref: jnp.concatenate([ref[t, rows, :] for t in range(GW // LANES)], axis=1)
            l4, l16 = tok(n4l_ref), tok(n16l_ref)
            m = jnp.maximum(jnp.maximum(lse, l4), l16)
            e1, e4, e16 = jnp.exp(lse - m), jnp.exp(l4 - m), jnp.exp(l16 - m)
            o_ref[rows, :] = (e1 * o + e4 * tok(n4o_ref) + e16 * tok(n16o_ref)) / (e1 + e4 + e16)
        else:
            o_ref[rows, :] = o
            l_ref[rows, :] = lse


def _dil_branch(qkv, batch, seq, dil, others=None):
    sub_len = seq // dil
    nj = sub_len // TJ_DIL
    col = lambda cb, rows, jmap: pl.BlockSpec((None, None, rows, GW), lambda b, r, j: (b, r, jmap(j), cb))
    in_specs = [col(0, TJ_DIL, lambda j: j), col(1, sub_len, lambda j: 0), col(2, sub_len, lambda j: 0)]
    out_blk = pl.BlockSpec((None, None, TJ_DIL, GW), lambda b, r, j: (b, r, j, 0))
    out_sd = jax.ShapeDtypeStruct((batch, dil, sub_len, GW), F32)
    args = [qkv, qkv, qkv]
    scratch = []
    if others is None:
        out_specs, out_shape = [out_blk, out_blk], [out_sd, out_sd]
    else:
        assert dil == 1
        for d in (4, 16):
            in_specs += [pl.BlockSpec((None, d, TJ_DIL // d, GW), lambda b, r, j: (b, 0, j, 0))] * 2
        args += list(others)
        out_specs, out_shape = out_blk, out_sd
        scratch = [pltpu.VMEM((GW // LANES, TJ_DIL, LANES), F32)] * 4
    return pl.pallas_call(
        functools.partial(_dil_kernel, sub_len=sub_len, merge=others is not None),
        grid=(batch, dil, nj),
        in_specs=in_specs,
        out_specs=out_specs,
        out_shape=out_shape,
        scratch_shapes=scratch,
        compiler_params=_params(("parallel", "parallel", "arbitrary")),
        name=f"dil_attn_d{dil}",
    )(*args)


def _reorder_w_in(w_in):
    qa = 0
    za = 4 * GW
    qb = za + 2 * GLA_RANK
    xc = qb + QKV
    qd = xc + 2 * GW
    end = qd + QKV
    pad = jnp.zeros(w_in.shape[:-1] + (LANES - 2 * GLA_RANK,), w_in.dtype)
    return jnp.concatenate([w_in[..., qa:za], w_in[..., xc:qd], w_in[..., za:qb], pad,
                            w_in[..., qb:xc], w_in[..., qd:end]], axis=-1).astype(BF16)


def _block_diag(w):
    out = jnp.zeros((GW, GW), w.dtype)
    for h in range(HEADS):
        out = out.at[h * HEAD_DIM:(h + 1) * HEAD_DIM, h * HEAD_DIM:(h + 1) * HEAD_DIM].set(w[h])
    return out


def _gate_weight(w_gate, direction):
    out = jnp.zeros((LANES, GW), F32)
    return out.at[direction * GLA_RANK:(direction + 1) * GLA_RANK].set(w_gate[direction]).astype(BF16)


def _rope_tables(seq):
    pos = jnp.arange(seq, dtype=F32)
    inv_freq = ROPE_THETA ** (-jnp.arange(0, HEAD_DIM, 2, dtype=F32) / HEAD_DIM)
    ang = pos[:, None] * inv_freq[None, :]
    cos, sin = jnp.cos(ang), jnp.sin(ang)
    return jnp.tile(jnp.concatenate([cos, cos], -1), (1, HEADS)), jnp.tile(jnp.concatenate([-sin, sin], -1), (1, HEADS))


def kernel(x, mix_norm_pre, mix_norm_post, w_in, gla_w_gate, gla_b_gate, gla_norm, na_rpb, lru_conv_w, lru_conv_b, lru_w_a, lru_b_a, lru_w_x, lru_b_x, lru_lambda, w_out, ffn_norm_pre, ffn_norm_post, ffn_w_in, ffn_w_out):
    batch, seq, d_model = x.shape
    assert d_model == D_MODEL and seq % (16 * TJ_DIL) == 0 and seq % TB_GLA == 0
    depth = w_in.shape[0]
    n = batch * seq
    xf = x.reshape(n, D_MODEL)
    cos_t, sin_t = _rope_tables(seq)
    w_in_r = _reorder_w_in(w_in)
    w_out_b = w_out.astype(BF16)
    ffn_w_in_b = ffn_w_in.astype(BF16)
    ffn_w_out_b = ffn_w_out.astype(BF16)
    row = lambda t: t.reshape(1, -1)

    for l in range(depth):
        pa, pb, d1, d4, d16 = _in_proj(xf, row(mix_norm_pre[l]), w_in_r[l], cos_t, sin_t, batch, seq)

        o_f = _gla_dir(pa, _gate_weight(gla_w_gate[l], 0), row(gla_b_gate[l, 0]), batch, seq, False)
        ya = _gla_dir(pa, _gate_weight(gla_w_gate[l], 1), row(gla_b_gate[l, 1]), batch, seq, True,
                      fin=(o_f, row(gla_norm[l])))

        yb = _nbr(pb, _nbr_bias(na_rpb[l]), batch, seq)

        lru_w = lambda e: dict(conv_w=lru_conv_w[l], conv_b=row(lru_conv_b[l]),
                               wa=_block_diag(lru_w_a[l, e]).astype(BF16), ba=row(lru_b_a[l, e]),
                               wx=_block_diag(lru_w_x[l, e]).astype(BF16), bx=row(lru_b_x[l, e]),
                               lam=row(lru_lambda[l, e]))
        h_f = _lru_dir(pa, lru_w(0), batch, seq, False)
        yc = _lru_dir(pa, lru_w(1), batch, seq, True, fin=h_f)

        o4, l4 = _dil_branch(d4, batch, seq, 4)
        o16, l16 = _dil_branch(d16, batch, seq, 16)
        yd = _dil_branch(d1.reshape(batch, 1, seq, QKV), batch, seq, 1, others=(o4, l4, o16, l16))

        xf = _out_proj(ya, yb, yc, yd.reshape(n, GW), w_out_b[l], xf, row(mix_norm_post[l]))
        xf = _ffn(xf, row(ffn_norm_pre[l]), ffn_w_in_b[l], ffn_w_out_b[l], row(ffn_norm_post[l]))
    return xf.reshape(batch, seq, D_MODEL)
```

```python
import functools

import numpy as np
import jax
import jax.numpy as jnp
from jax import lax
from jax.experimental import pallas as pl
from jax.experimental.pallas import tpu as pltpu

F32 = jnp.float32
BF16 = jnp.bfloat16

D_MODEL = 1024
HEAD_DIM = 64
HEADS = 4
GW = HEADS * HEAD_DIM
GLA_RANK = 16
GLA_TAU = 16.0
GLA_CHUNK = 64
GRID_W = 64
NA_ROWS = 8
NA_COLS = 16
LRU_C = 8.0
LRU_CONV = 4
DILATIONS = (1, 4, 16)
DIL_RADIUS = 64
ROPE_THETA = 10000.0
D_FF = 2816
EPS = 1e-6
QK_SCALE = HEAD_DIM ** -0.5
LANES = 128

CB_QA, CB_KA, CB_VA, CB_GA, CB_XC, CB_GC = 0, 1, 2, 3, 4, 5
Z_COL0 = 6 * GW
PA_COLS = Z_COL0 + LANES
QKV = 3 * GW
P_COLS = PA_COLS + 2 * QKV

VMEM_LIMIT = 56 * 1024 * 1024


def _params(sem, vmem=VMEM_LIMIT):
    return pltpu.CompilerParams(dimension_semantics=sem, vmem_limit_bytes=vmem)


def _head_mask(rows, h):
    lane = lax.broadcasted_iota(jnp.int32, (rows, GW), 1)
    return (lane >> 6) == h


def _stack_heads(t):
    rows = t.shape[0]
    zero = jnp.zeros_like(t)
    return jnp.concatenate([jnp.where(_head_mask(rows, h), t, zero) for h in range(HEADS)], axis=0)


def _unstack_heads(t):
    rows = t.shape[0] // HEADS
    out = jnp.where(_head_mask(rows, 0), t[0:rows], 0.0)
    for h in range(1, HEADS):
        out = out + jnp.where(_head_mask(rows, h), t[h * rows:(h + 1) * rows], 0.0)
    return out


def _dot(a, b):
    return jnp.dot(a, b, preferred_element_type=F32)


def _dot_nt(a, b):
    return lax.dot_general(a, b, (((1,), (1,)), ((), ())), preferred_element_type=F32)


def _dot_tn(a, b):
    return lax.dot_general(a, b, (((0,), (0,)), ((), ())), preferred_element_type=F32)


def _split(t):
    hi = t.astype(BF16)
    return hi, (t - hi.astype(F32)).astype(BF16)


def _split_dot(m, t):
    hi, lo = _split(t)
    return _dot(m, hi) + _dot(m, lo)


def _split_dot_rhs(t, m):
    hi, lo = _split(t)
    return _dot(hi, m) + _dot(lo, m)


def _rms(x, g):
    ms = jnp.mean(x * x, axis=-1, keepdims=True)
    return x * lax.rsqrt(ms + EPS) * g


def _sigmoid(x):
    return 1.0 / (1.0 + jnp.exp(-x))


def _softplus(x):
    return jnp.maximum(x, 0.0) + jnp.log1p(jnp.exp(-jnp.abs(x)))


def _softmax_pv(s, v):
    m = jnp.max(s, axis=-1, keepdims=True)
    e = jnp.exp(s - m).astype(BF16)
    pv = _dot(e, jnp.concatenate([v, jnp.ones((v.shape[0], LANES), BF16)], axis=1))
    den = _unstack_heads(jnp.concatenate([pv[:, GW:], pv[:, GW:]], axis=1))
    o = _unstack_heads(pv[:, :GW]) / den
    lse = _unstack_heads(jnp.broadcast_to(m, (s.shape[0], GW))) + jnp.log(den)
    return o, lse


TM_IN = 512


def _rope(t, c, s):
    lane = lax.broadcasted_iota(jnp.int32, t.shape, 1)
    first_half = (lane & (HEAD_DIM - 1)) < HEAD_DIM // 2
    swapped = jnp.where(first_half, pltpu.roll(t, GW - HEAD_DIM // 2, 1), pltpu.roll(t, HEAD_DIM // 2, 1))
    return t * c + swapped * s


def _in_proj_kernel(x_ref, g_ref, w_ref, c_ref, s_ref, pa_ref, pb_ref, d1_ref, d4_ref, d16_ref, rope_ref):
    h = _rms(x_ref[...], g_ref[...]).astype(BF16)
    pa_ref[...] = _dot(h, w_ref[:, 0:PA_COLS])
    pb = _dot(h, w_ref[:, PA_COLS:PA_COLS + QKV])
    pb_ref[:, 0:GW] = (pb[:, 0:GW] * QK_SCALE).astype(BF16)
    pb_ref[:, GW:QKV] = pb[:, GW:QKV].astype(BF16)
    pd = _dot(h, w_ref[:, PA_COLS + QKV:P_COLS])
    c = c_ref[...]
    s = s_ref[...]
    qkv = jnp.concatenate([_rope(pd[:, 0:GW], c, s) * QK_SCALE, _rope(pd[:, GW:2 * GW], c, s),
                           pd[:, 2 * GW:QKV]], axis=1)
    d1_ref[...] = qkv.astype(BF16)
    for t in range(QKV // LANES):
        lanes = slice(t * LANES, (t + 1) * LANES)
        rope_ref[t] = qkv[:, lanes]
        for d, ref in ((4, d4_ref), (16, d16_ref)):
            for r in range(d):
                ref[r, :, lanes] = rope_ref[t, pl.ds(r, TM_IN // d, stride=d), :].astype(BF16)


def _in_proj(x, w, l, cos_t, sin_t, batch, seq):
    n = x.shape[0]
    nl = seq // TM_IN
    row = lambda cols: pl.BlockSpec((TM_IN, cols), lambda i: (i, 0))
    tab = pl.BlockSpec((TM_IN, GW), lambda i: (i % nl, 0))
    res = lambda d: pl.BlockSpec((None, d, TM_IN // d, QKV), lambda i: (i // nl, 0, i % nl, 0))
    return pl.pallas_call(
        _in_proj_kernel,
        grid=(n // TM_IN,),
        in_specs=[row(D_MODEL),
                  pl.BlockSpec((None, 1, D_MODEL), lambda i: (l, 0, 0)),
                  pl.BlockSpec((None, D_MODEL, P_COLS), lambda i: (l, 0, 0)),
                  tab, tab],
        out_specs=[row(PA_COLS), row(QKV), row(QKV), res(4), res(16)],
        out_shape=[jax.ShapeDtypeStruct((n, PA_COLS), F32),
                   jax.ShapeDtypeStruct((n, QKV), BF16),
                   jax.ShapeDtypeStruct((n, QKV), BF16),
                   jax.ShapeDtypeStruct((batch, 4, seq // 4, QKV), BF16),
                   jax.ShapeDtypeStruct((batch, 16, seq // 16, QKV), BF16)],
        scratch_shapes=[pltpu.VMEM((QKV // LANES, TM_IN, LANES), F32)],
        compiler_params=_params(("parallel",)),
        name="in_proj",
    )(x, w["mix_pre"], w["w_in"], cos_t, sin_t)


TM_FFN = 512
TF_FFN = 1408


def _tail_kernel(ya_ref, yb_ref, yc_ref, yd_ref, wm_ref, x_ref, gm_ref, g1_ref, wg_ref, wu_ref, wo_ref, g2_ref,
                 o_ref, x1_ref, h_ref, acc_ref):
    j = pl.program_id(1)

    @pl.when(j == 0)
    def _():
        y = _dot(ya_ref[...].astype(BF16), wm_ref[0 * GW:1 * GW, :])
        y += _dot(yb_ref[...].astype(BF16), wm_ref[1 * GW:2 * GW, :])
        y += _dot(yc_ref[...].astype(BF16), wm_ref[2 * GW:3 * GW, :])
        y += _dot(yd_ref[...].astype(BF16), wm_ref[3 * GW:4 * GW, :])
        x1 = x_ref[...] + _rms(y, gm_ref[...])
        x1_ref[...] = x1
        h_ref[...] = _rms(x1, g1_ref[...]).astype(BF16)
        acc_ref[...] = jnp.zeros_like(acc_ref)

    h = h_ref[...]
    gate = _dot(h, wg_ref[...])
    up = _dot(h, wu_ref[...])
    act = (gate * _sigmoid(gate) * up).astype(BF16)
    acc_ref[...] += _dot(act, wo_ref[...])

    @pl.when(j == pl.num_programs(1) - 1)
    def _():
        o_ref[...] = x1_ref[...] + _rms(acc_ref[...], g2_ref[...])


def _tail(ya, yb, yc, yd, x, w, l):
    n = x.shape[0]
    nf = D_FF // TF_FFN
    row = pl.BlockSpec((TM_FFN, GW), lambda i, j: (i, 0))
    gain = pl.BlockSpec((None, 1, D_MODEL), lambda i, j: (l, 0, 0))
    return pl.pallas_call(
        _tail_kernel,
        grid=(n // TM_FFN, nf),
        in_specs=[row] * 4 + [
            pl.BlockSpec((None, D_MODEL, D_MODEL), lambda i, j: (l, 0, 0)),
            pl.BlockSpec((TM_FFN, D_MODEL), lambda i, j: (i, 0)),
            gain, gain,
            pl.BlockSpec((None, D_MODEL, TF_FFN), lambda i, j: (l, 0, j)),
            pl.BlockSpec((None, D_MODEL, TF_FFN), lambda i, j: (l, 0, nf + j)),
            pl.BlockSpec((None, TF_FFN, D_MODEL), lambda i, j: (l, j, 0)),
            gain,
        ],
        out_specs=pl.BlockSpec((TM_FFN, D_MODEL), lambda i, j: (i, 0)),
        out_shape=jax.ShapeDtypeStruct((n, D_MODEL), F32),
        scratch_shapes=[pltpu.VMEM((TM_FFN, D_MODEL), F32), pltpu.VMEM((TM_FFN, D_MODEL), BF16),
                        pltpu.VMEM((TM_FFN, D_MODEL), F32)],
        compiler_params=_params(("parallel", "arbitrary")),
        name="out_proj_ffn",
    )(ya, yb, yc, yd, w["w_out"], x, w["mix_post"], w["ffn_pre"], w["ffn_in"], w["ffn_in"], w["ffn_out"],
      w["ffn_post"])


TB_GLA = 512


def _gla_kernel(*refs, reverse, finalize):
    if finalize:
        q_ref, k_ref, v_ref, z_ref, wg_ref, bg_ref, of_ref, g_ref, ng_ref, o_ref, st_ref = refs
    else:
        q_ref, k_ref, v_ref, z_ref, wg_ref, bg_ref, o_ref, st_ref = refs
    C = GLA_CHUNK
    n_chunks = TB_GLA // C

    @pl.when(pl.program_id(1) == 0)
    def _():
        st_ref[...] = jnp.zeros_like(st_ref)

    logit = _dot(z_ref[...].astype(BF16), wg_ref[...]) + bg_ref[...]
    log_a = -_softplus(-logit) * (1.0 / GLA_TAU)

    r_i = lax.broadcasted_iota(jnp.int32, (C, C), 0)
    c_i = lax.broadcasted_iota(jnp.int32, (C, C), 1)
    cum = jnp.where((c_i >= r_i) if reverse else (c_i <= r_i), 1.0, 0.0).astype(BF16)
    t_i = lax.broadcasted_iota(jnp.int32, (C, GW), 0)
    s_i = lax.broadcasted_iota(jnp.int32, (C, GW), 1) & (C - 1)
    causal = (s_i >= t_i) if reverse else (s_i <= t_i)
    rb = lax.broadcasted_iota(jnp.int32, (GW, GW), 0) >> 6
    cb = lax.broadcasted_iota(jnp.int32, (GW, GW), 1) >> 6
    same_head = rb == cb
    last, mid = (0, C // 2) if reverse else (C - 1, C // 2 - 1)

    b_c = [_split_dot(cum, log_a[c * C:(c + 1) * C]) for c in range(n_chunks)]
    b = jnp.concatenate(b_c, axis=0)
    b_last = jnp.concatenate([jnp.broadcast_to(t[last:last + 1], (C, GW)) for t in b_c], axis=0)
    b_mid = jnp.concatenate([jnp.broadcast_to(t[mid:mid + 1], (C, GW)) for t in b_c], axis=0)
    q = q_ref[...] * QK_SCALE
    k = k_ref[...]
    v = v_ref[...].astype(BF16)
    q_in = (q * jnp.exp(b - b_mid)).astype(BF16)
    k_in = (k * jnp.exp(b_mid - b)).astype(BF16)
    q_st = (q * jnp.exp(b)).astype(BF16)
    k_st = (k * jnp.exp(b_last - b)).astype(BF16)

    order = range(n_chunks - 1, -1, -1) if reverse else range(n_chunks)
    for c in order:
        rows = slice(c * C, (c + 1) * C)
        v_c = v[rows]
        att = _dot_nt(q_in[rows], _stack_heads(k_in[rows]))
        o = _dot(jnp.where(causal, att, 0.0).astype(BF16), _stack_heads(v_c))
        st = st_ref[...]
        o = o + _dot_nt(q_st[rows], st.astype(BF16))
        upd = _dot_tn(v_c, k_st[rows])
        st_ref[...] = st * jnp.exp(b_c[c][last:last + 1]) + jnp.where(same_head, upd, 0.0)
        o_ref[rows, :] = o

    if finalize:
        o = o_ref[...] + of_ref[...]
        blk = jnp.where(same_head, 1.0 / HEAD_DIM, 0.0).astype(BF16)
        ms = _split_dot_rhs(o * o, blk)
        g = g_ref[...]
        o_ref[...] = o * lax.rsqrt(ms + EPS) * ng_ref[...] * (g * _sigmoid(g))


def _gla_dir(pa, w, l, batch, seq, reverse, o_f=None):
    nb = seq // TB_GLA
    n = batch * seq
    e = 1 if reverse else 0

    def rb(b, i):
        return b * nb + ((nb - 1 - i) if reverse else i)

    col = lambda cb: pl.BlockSpec((TB_GLA, GW), lambda b, i: (rb(b, i), cb))
    in_specs = [col(CB_QA), col(CB_KA), col(CB_VA),
                pl.BlockSpec((TB_GLA, LANES), lambda b, i: (rb(b, i), Z_COL0 // LANES)),
                pl.BlockSpec((None, None, LANES, GW), lambda b, i: (l, e, 0, 0)),
                pl.BlockSpec((None, None, 1, GW), lambda b, i: (l, e, 0, 0))]
    args = [pa, pa, pa, pa, w["gla_wg"], w["gla_bg"]]
    if o_f is not None:
        in_specs += [pl.BlockSpec((TB_GLA, GW), lambda b, i: (rb(b, i), 0)), col(CB_GA),
                     pl.BlockSpec((None, 1, GW), lambda b, i: (l, 0, 0))]
        args += [o_f, pa, w["gla_norm"]]
    return pl.pallas_call(
        functools.partial(_gla_kernel, reverse=reverse, finalize=o_f is not None),
        grid=(batch, nb),
        in_specs=in_specs,
        out_specs=pl.BlockSpec((TB_GLA, GW), lambda b, i: (rb(b, i), 0)),
        out_shape=jax.ShapeDtypeStruct((n, GW), F32),
        scratch_shapes=[pltpu.VMEM((GW, GW), F32)],
        compiler_params=_params(("arbitrary", "arbitrary")),
        name="gla_bwd" if reverse else "gla_fwd",
    )(*args)


NA_BLOCK_ROWS = 8
NA_PAIRS = 2 * NA_ROWS - 2


def _nbr_kernel(q_ref, k_ref, v_ref, bias_ref, o_ref, *, grid_rows):
    W = GRID_W
    win = NA_ROWS * W
    row0 = pl.program_id(1) * NA_BLOCK_ROWS

    def body(j, carry):
        r = row0 + j
        start = jnp.clip(r - NA_ROWS // 2, 0, grid_rows - NA_ROWS)
        pat = start - r + (NA_ROWS - 1)
        k0 = pl.multiple_of(start * W, W)
        q0 = pl.multiple_of(j * W, W)
        s = _dot_nt(_stack_heads(q_ref[pl.ds(q0, W), :]), k_ref[pl.ds(k0, win), :])
        s = s + jnp.concatenate([bias_ref[pat + 2 * t] for t in range(NA_ROWS // 2)], axis=1)
        o, _ = _softmax_pv(s, v_ref[pl.ds(k0, win), :])
        o_ref[pl.ds(q0, W), :] = o
        return carry

    lax.fori_loop(0, NA_BLOCK_ROWS, body, 0, unroll=2)


def _nbr(pb, w, l, batch, seq):
    grid_rows = seq // GRID_W
    tq = NA_BLOCK_ROWS * GRID_W
    nb = seq // tq
    return pl.pallas_call(
        functools.partial(_nbr_kernel, grid_rows=grid_rows),
        grid=(batch, nb),
        in_specs=[
            pl.BlockSpec((tq, GW), lambda b, i: (b * nb + i, 0)),
            pl.BlockSpec((seq, GW), lambda b, i: (b, 1)),
            pl.BlockSpec((seq, GW), lambda b, i: (b, 2)),
            pl.BlockSpec((None, NA_PAIRS, HEADS * GRID_W, 2 * GRID_W), lambda b, i: (l, 0, 0, 0)),
        ],
        out_specs=pl.BlockSpec((tq, GW), lambda b, i: (b * nb + i, 0)),
        out_shape=jax.ShapeDtypeStruct((batch * seq, GW), F32),
        compiler_params=_params(("parallel", "arbitrary")),
        name="nbr_attn",
    )(pb, pb, pb, w["nbr_bias"])


def _nbr_bias(rpb):
    depth = rpb.shape[0]
    c = np.arange(GRID_W)
    dc = np.clip(c[None, :] - c[:, None], -(NA_COLS - 1), NA_COLS - 1) + NA_COLS - 1
    onehot = jnp.asarray(dc.reshape(-1)[None, :] == np.arange(2 * NA_COLS - 1)[:, None], F32)
    t = jnp.dot(rpb.reshape(-1, 2 * NA_COLS - 1), onehot, precision=lax.Precision.HIGHEST)
    t = t.reshape(depth, HEADS, 2 * NA_ROWS - 1, GRID_W, GRID_W)
    col_start = np.clip(c - NA_COLS // 2, 0, GRID_W - NA_COLS)
    col_ok = (c[None, :] >= col_start[:, None]) & (c[None, :] < col_start[:, None] + NA_COLS)
    t = jnp.where(jnp.asarray(col_ok), t, -jnp.inf)
    t = t.transpose(0, 2, 1, 3, 4).reshape(depth, 2 * NA_ROWS - 1, HEADS * GRID_W, GRID_W)
    return jnp.concatenate([t[:, :-1], t[:, 1:]], axis=-1)


TB_LRU = 512
SUB = 8


def _gelu_tanh(x):
    return 0.5 * x * (1.0 + jnp.tanh(0.7978845608028654 * (x + 0.044715 * x * x * x)))


def _lru_kernel(*refs, reverse, finalize):
    if finalize:
        (x_ref, xp_ref, xn_ref, cw_ref, cb_ref, wa_ref, ba_ref, wx_ref, bx_ref, lam_ref,
         hf_ref, gate_ref, o_ref, ext_ref, a_ref, u_ref, carry_ref) = refs
    else:
        (x_ref, xp_ref, xn_ref, cw_ref, cb_ref, wa_ref, ba_ref, wx_ref, bx_ref, lam_ref,
         o_ref, ext_ref, a_ref, u_ref, carry_ref) = refs
    TB = TB_LRU
    i = pl.program_id(1)
    nb = pl.num_programs(1)
    seq_blk = (nb - 1 - i) if reverse else i

    @pl.when(i == 0)
    def _():
        carry_ref[...] = jnp.zeros_like(carry_ref)

    ext_ref[0:SUB, :] = jnp.where(seq_blk == 0, 0.0, xp_ref[...])
    ext_ref[SUB:SUB + TB, :] = x_ref[...]
    ext_ref[SUB + TB:, :] = jnp.where(seq_blk == nb - 1, 0.0, xn_ref[...])
    xc = cb_ref[...] + jnp.zeros((TB, GW), F32)
    for j in range(LRU_CONV):
        xc = xc + ext_ref[pl.ds(SUB - 2 + j, TB), :] * cw_ref[j:j + 1, :]

    xcb = xc.astype(BF16)
    r = _sigmoid(_dot(xcb, wa_ref[...]) + ba_ref[...])
    gi = _sigmoid(_dot(xcb, wx_ref[...]) + bx_ref[...])
    log_a = (-LRU_C) * r * _softplus(-lam_ref[...])
    a = jnp.exp(log_a)
    u = jnp.sqrt(-jnp.tanh(log_a) * (a * a + 1.0)) * (gi * xc)

    rm = lax.broadcasted_iota(jnp.int32, (TB, GW), 0) & (SUB - 1)
    for d in (1, 2, 4):
        if reverse:
            a_sh, u_sh, ok = pltpu.roll(a, TB - d, 0), pltpu.roll(u, TB - d, 0), rm < SUB - d
        else:
            a_sh, u_sh, ok = pltpu.roll(a, d, 0), pltpu.roll(u, d, 0), rm >= d
        u = jnp.where(ok, a * u_sh + u, u)
        a = jnp.where(ok, a * a_sh, a)
    a_ref[...] = a
    u_ref[...] = u

    nt = TB // SUB
    edge = 0 if reverse else SUB - 1

    def body(t, carry):
        off = pl.multiple_of(((nt - 1 - t) if reverse else t) * SUB, SUB)
        h = u_ref[pl.ds(off, SUB), :] + a_ref[pl.ds(off, SUB), :] * carry
        u_ref[pl.ds(off, SUB), :] = h
        return jnp.broadcast_to(h[edge:edge + 1, :], (SUB, GW))

    carry_ref[...] = lax.fori_loop(0, nt, body, carry_ref[...])

    h = u_ref[...]
    if finalize:
        h = (h + hf_ref[...]) * _gelu_tanh(gate_ref[...])
    o_ref[...] = h


def _lru_dir(pa, w, l, batch, seq, reverse, h_f=None):
    nb = seq // TB_LRU
    n = batch * seq
    tiles = TB_LRU // SUB
    e = 1 if reverse else 0

    def rb(b, i):
        return b * nb + ((nb - 1 - i) if reverse else i)

    per_layer = lambda rows: pl.BlockSpec((None, rows, GW), lambda b, i: (l, 0, 0))
    per_dir = lambda rows: pl.BlockSpec((None, None, rows, GW), lambda b, i: (l, e, 0, 0))
    in_specs = [
        pl.BlockSpec((TB_LRU, GW), lambda b, i: (rb(b, i), CB_XC)),
        pl.BlockSpec((SUB, GW), lambda b, i: (jnp.maximum(rb(b, i) * tiles - 1, 0), CB_XC)),
        pl.BlockSpec((SUB, GW), lambda b, i: (jnp.minimum((rb(b, i) + 1) * tiles, n // SUB - 1), CB_XC)),
        per_layer(LRU_CONV), per_layer(1), per_dir(GW), per_dir(1), per_dir(GW), per_dir(1), per_dir(1),
    ]
    args = [pa, pa, pa, w["lru_conv_w"], w["lru_conv_b"], w["lru_wa"], w["lru_ba"], w["lru_wx"], w["lru_bx"],
            w["lru_lam"]]
    if h_f is not None:
        in_specs += [pl.BlockSpec((TB_LRU, GW), lambda b, i: (rb(b, i), 0)),
                     pl.BlockSpec((TB_LRU, GW), lambda b, i: (rb(b, i), CB_GC))]
        args += [h_f, pa]
    return pl.pallas_call(
        functools.partial(_lru_kernel, reverse=reverse, finalize=h_f is not None),
        grid=(batch, nb),
        in_specs=in_specs,
        out_specs=pl.BlockSpec((TB_LRU, GW), lambda b, i: (rb(b, i), 0)),
        out_shape=jax.ShapeDtypeStruct((n, GW), F32),
        scratch_shapes=[pltpu.VMEM((TB_LRU + 2 * SUB, GW), F32), pltpu.VMEM((TB_LRU, GW), F32),
                        pltpu.VMEM((TB_LRU, GW), F32), pltpu.VMEM((SUB, GW), F32)],
        compiler_params=_params(("arbitrary", "arbitrary")),
        name="lru_bwd" if reverse else "lru_fwd",
    )(*args)


TJ_DIL = 512
SB_DIL = 128
WIN_DIL = SB_DIL + 2 * DIL_RADIUS


def _band_masks():
    q = np.arange(HEADS * SB_DIL)[:, None] % SB_DIL
    c = np.arange(WIN_DIL)[None, :]
    offs = (0, -DIL_RADIUS, -2 * DIL_RADIUS)
    return jnp.asarray(np.stack([np.where(np.abs(c + off - q) <= DIL_RADIUS, 0.0, -np.inf) for off in offs]), F32)


def _dil_kernel(*refs, sub_len, merge):
    if merge:
        (q_ref, k_ref, v_ref, mask_ref, o4_ref, l4_ref, o16_ref, l16_ref, o_ref,
         n4o_ref, n4l_ref, n16o_ref, n16l_ref) = refs
        for d, src_o, src_l, dst_o, dst_l in ((4, o4_ref, l4_ref, n4o_ref, n4l_ref),
                                              (16, o16_ref, l16_ref, n16o_ref, n16l_ref)):
            for r in range(d):
                for t in range(GW // LANES):
                    lanes = slice(t * LANES, (t + 1) * LANES)
                    dst_o[t, pl.ds(r, TJ_DIL // d, stride=d), :] = src_o[r, :, lanes]
                    dst_l[t, pl.ds(r, TJ_DIL // d, stride=d), :] = src_l[r, :, lanes]
    else:
        q_ref, k_ref, v_ref, mask_ref, o_ref, l_ref = refs
    j_blk = pl.program_id(2) * TJ_DIL
    for sb in range(TJ_DIL // SB_DIL):
        rows = slice(sb * SB_DIL, (sb + 1) * SB_DIL)
        j0 = j_blk + sb * SB_DIL
        ws = pl.multiple_of(jnp.clip(j0 - DIL_RADIUS, 0, sub_len - WIN_DIL), DIL_RADIUS)
        which = 1 - (j0 == 0).astype(jnp.int32) + (j0 == sub_len - SB_DIL).astype(jnp.int32)
        s = _dot_nt(_stack_heads(q_ref[rows, :]), k_ref[pl.ds(ws, WIN_DIL), :])
        o, lse = _softmax_pv(s + mask_ref[which], v_ref[pl.ds(ws, WIN_DIL), :])
        if merge:
            tok = lambda ref: jnp.concatenate([ref[t, rows, :] for t in range(GW // LANES)], axis=1)
            l4, l16 = tok(n4l_ref), tok(n16l_ref)
            m = jnp.maximum(jnp.maximum(lse, l4), l16)
            e1, e4, e16 = jnp.exp(lse - m), jnp.exp(l4 - m), jnp.exp(l16 - m)
            o_ref[rows, :] = (e1 * o + e4 * tok(n4o_ref) + e16 * tok(n16o_ref)) / (e1 + e4 + e16)
        else:
            o_ref[rows, :] = o
            l_ref[rows, :] = lse


def _dil_branch(qkv, masks, batch, seq, dil, others=None):
    sub_len = seq // dil
    nj = sub_len // TJ_DIL
    col = lambda cb, rows, jmap: pl.BlockSpec((None, None, rows, GW), lambda b, r, j: (b, r, jmap(j), cb))
    in_specs = [col(0, TJ_DIL, lambda j: j), col(1, sub_len, lambda j: 0), col(2, sub_len, lambda j: 0),
                pl.BlockSpec(masks.shape, lambda b, r, j: (0, 0, 0))]
    out_blk = pl.BlockSpec((None, None, TJ_DIL, GW), lambda b, r, j: (b, r, j, 0))
    out_sd = jax.ShapeDtypeStruct((batch, dil, sub_len, GW), F32)
    args = [qkv, qkv, qkv, masks]
    scratch = []
    if others is None:
        out_specs, out_shape = [out_blk, out_blk], [out_sd, out_sd]
    else:
        assert dil == 1
        for d in (4, 16):
            in_specs += [pl.BlockSpec((None, d, TJ_DIL // d, GW), lambda b, r, j: (b, 0, j, 0))] * 2
        args += list(others)
        out_specs, out_shape = out_blk, out_sd
        scratch = [pltpu.VMEM((GW // LANES, TJ_DIL, LANES), F32)] * 4
    return pl.pallas_call(
        functools.partial(_dil_kernel, sub_len=sub_len, merge=others is not None),
        grid=(batch, dil, nj),
        in_specs=in_specs,
        out_specs=out_specs,
        out_shape=out_shape,
        scratch_shapes=scratch,
        compiler_params=_params(("parallel", "parallel", "arbitrary")),
        name=f"dil_attn_d{dil}",
    )(*args)


def _reorder_w_in(w_in):
    qa = 0
    za = 4 * GW
    qb = za + 2 * GLA_RANK
    xc = qb + QKV
    qd = xc + 2 * GW
    end = qd + QKV
    pad = jnp.zeros(w_in.shape[:-1] + (LANES - 2 * GLA_RANK,), w_in.dtype)
    return jnp.concatenate([w_in[..., qa:za], w_in[..., xc:qd], w_in[..., za:qb], pad,
                            w_in[..., qb:xc], w_in[..., qd:end]], axis=-1).astype(BF16)


def _block_diag(w):
    eye = jnp.eye(HEADS, dtype=w.dtype)[:, None, :, None]
    out = w[..., :, :, None, :] * eye
    return out.reshape(w.shape[:-3] + (GW, GW))


def _gate_weights(w_gate):
    per_dir = [jnp.pad(w_gate[:, e], ((0, 0), (e * GLA_RANK, LANES - (e + 1) * GLA_RANK), (0, 0))) for e in (0, 1)]
    return jnp.stack(per_dir, axis=1).astype(BF16)


def _rope_tables(seq):
    pos = jnp.arange(seq, dtype=F32)
    inv_freq = ROPE_THETA ** (-jnp.arange(0, HEAD_DIM, 2, dtype=F32) / HEAD_DIM)
    ang = pos[:, None] * inv_freq[None, :]
    cos, sin = jnp.cos(ang), jnp.sin(ang)
    return jnp.tile(jnp.concatenate([cos, cos], -1), (1, HEADS)), jnp.tile(jnp.concatenate([-sin, sin], -1), (1, HEADS))


def kernel(x, mix_norm_pre, mix_norm_post, w_in, gla_w_gate, gla_b_gate, gla_norm, na_rpb, lru_conv_w, lru_conv_b, lru_w_a, lru_b_a, lru_w_x, lru_b_x, lru_lambda, w_out, ffn_norm_pre, ffn_norm_post, ffn_w_in, ffn_w_out):
    batch, seq, d_model = x.shape
    assert d_model == D_MODEL and seq % (16 * TJ_DIL) == 0 and seq % TB_GLA == 0
    depth = w_in.shape[0]
    n = batch * seq
    xf = x.reshape(n, D_MODEL)
    cos_t, sin_t = _rope_tables(seq)
    masks = _band_masks()
    rows = lambda t: t[..., None, :]
    w = dict(
        mix_pre=rows(mix_norm_pre), mix_post=rows(mix_norm_post), w_in=_reorder_w_in(w_in),
        gla_wg=_gate_weights(gla_w_gate), gla_bg=rows(gla_b_gate), gla_norm=rows(gla_norm),
        nbr_bias=_nbr_bias(na_rpb),
        lru_conv_w=lru_conv_w, lru_conv_b=rows(lru_conv_b),
        lru_wa=_block_diag(lru_w_a).astype(BF16), lru_ba=rows(lru_b_a),
        lru_wx=_block_diag(lru_w_x).astype(BF16), lru_bx=rows(lru_b_x), lru_lam=rows(lru_lambda),
        w_out=w_out.astype(BF16), ffn_pre=rows(ffn_norm_pre), ffn_post=rows(ffn_norm_post),
        ffn_in=ffn_w_in.astype(BF16), ffn_out=ffn_w_out.astype(BF16),
    )

    for l in range(depth):
        pa, pb, d1, d4, d16 = _in_proj(xf, w, l, cos_t, sin_t, batch, seq)

        o_f = _gla_dir(pa, w, l, batch, seq, False)
        ya = _gla_dir(pa, w, l, batch, seq, True, o_f=o_f)

        yb = _nbr(pb, w, l, batch, seq)

        h_f = _lru_dir(pa, w, l, batch, seq, False)
        yc = _lru_dir(pa, w, l, batch, seq, True, h_f=h_f)

        o4, l4 = _dil_branch(d4, masks, batch, seq, 4)
        o16, l16 = _dil_branch(d16, masks, batch, seq, 16)
        yd = _dil_branch(d1.reshape(batch, 1, seq, QKV), masks, batch, seq, 1, others=(o4, l4, o16, l16))

        xf = _tail(ya, yb, yc, yd.reshape(n, GW), xf, w, l)
    return xf.reshape(batch, seq, D_MODEL)
```

```python
import functools

import numpy as np
import jax
import jax.numpy as jnp
from jax import lax
from jax.experimental import pallas as pl
from jax.experimental.pallas import tpu as pltpu

F32 = jnp.float32
BF16 = jnp.bfloat16

D_MODEL = 1024
HEAD_DIM = 64
HEADS = 4
GW = HEADS * HEAD_DIM
GLA_RANK = 16
GLA_TAU = 16.0
GLA_CHUNK = 64
GRID_W = 64
NA_ROWS = 8
NA_COLS = 16
LRU_C = 8.0
LRU_CONV = 4
DILATIONS = (1, 4, 16)
DIL_RADIUS = 64
ROPE_THETA = 10000.0
D_FF = 2816
EPS = 1e-6
QK_SCALE = HEAD_DIM ** -0.5
LANES = 128

CB_QA, CB_KA, CB_VA, CB_GA, CB_XC, CB_GC = 0, 1, 2, 3, 4, 5
Z_COL0 = 6 * GW
PA_COLS = Z_COL0 + LANES
QKV = 3 * GW
P_COLS = PA_COLS + 2 * QKV

VMEM_LIMIT = 56 * 1024 * 1024


def _params(sem, vmem=VMEM_LIMIT):
    return pltpu.CompilerParams(dimension_semantics=sem, vmem_limit_bytes=vmem)


def _head_mask(rows, h):
    lane = lax.broadcasted_iota(jnp.int32, (rows, GW), 1)
    return (lane >> 6) == h


def _stack_heads(t):
    rows = t.shape[0]
    lane = lax.broadcasted_iota(jnp.int32, (rows, LANES), 1)
    zero = jnp.zeros((rows, LANES), t.dtype)
    blocks = []
    for h in range(HEADS):
        half = t[:, (h // 2) * LANES:(h // 2 + 1) * LANES]
        half = jnp.where((lane < HEAD_DIM) if h % 2 == 0 else (lane >= HEAD_DIM), half, zero)
        blocks.append(jnp.concatenate([half, zero] if h < 2 else [zero, half], axis=1))
    return jnp.concatenate(blocks, axis=0)


def _unstack_heads(t):
    rows = t.shape[0] // HEADS
    first = lax.broadcasted_iota(jnp.int32, (rows, LANES), 1) < HEAD_DIM
    blk = lambda h, tile: t[h * rows:(h + 1) * rows, tile * LANES:(tile + 1) * LANES]
    return jnp.concatenate([jnp.where(first, blk(0, 0), blk(1, 0)), jnp.where(first, blk(2, 1), blk(3, 1))], axis=1)


def _dot(a, b):
    return jnp.dot(a, b, preferred_element_type=F32)


def _dot_nt(a, b):
    return lax.dot_general(a, b, (((1,), (1,)), ((), ())), preferred_element_type=F32)


def _dot_tn(a, b):
    return lax.dot_general(a, b, (((0,), (0,)), ((), ())), preferred_element_type=F32)


def _split(t):
    hi = t.astype(BF16)
    return hi, (t - hi.astype(F32)).astype(BF16)


def _split_dot(m, t):
    hi, lo = _split(t)
    return _dot(m, hi) + _dot(m, lo)


def _split_dot_rhs(t, m):
    hi, lo = _split(t)
    return _dot(hi, m) + _dot(lo, m)


def _rms(x, g):
    ms = jnp.mean(x * x, axis=-1, keepdims=True)
    return x * lax.rsqrt(ms + EPS) * g


def _sigmoid(x):
    return 1.0 / (1.0 + jnp.exp(-x))


def _softplus(x):
    return jnp.maximum(x, 0.0) + jnp.log1p(jnp.exp(-jnp.abs(x)))


def _softmax_pv(s, v):
    m = jnp.max(s, axis=-1, keepdims=True)
    e = jnp.exp(s - m).astype(BF16)
    pv = _dot(e, jnp.concatenate([v, jnp.ones((v.shape[0], LANES), BF16)], axis=1))
    den = _unstack_heads(jnp.concatenate([pv[:, GW:], pv[:, GW:]], axis=1))
    o = _unstack_heads(pv[:, :GW]) / den
    lse = _unstack_heads(jnp.broadcast_to(m, (s.shape[0], GW))) + jnp.log(den)
    return o, lse


TM_IN = 512


def _rope(t, c, s):
    lane = lax.broadcasted_iota(jnp.int32, t.shape, 1)
    first_half = (lane & (HEAD_DIM - 1)) < HEAD_DIM // 2
    swapped = jnp.where(first_half, pltpu.roll(t, GW - HEAD_DIM // 2, 1), pltpu.roll(t, HEAD_DIM // 2, 1))
    return t * c + swapped * s


def _in_proj_kernel(x_ref, g_ref, w_ref, c_ref, s_ref, pa_ref, pb_ref, d1_ref, d4_ref, d16_ref, rope_ref):
    h = _rms(x_ref[...], g_ref[...]).astype(BF16)
    pd = _dot(h, w_ref[:, PA_COLS + QKV:P_COLS])
    pb = _dot(h, w_ref[:, PA_COLS:PA_COLS + QKV])
    pb_ref[:, 0:GW] = (pb[:, 0:GW] * QK_SCALE).astype(BF16)
    pb_ref[:, GW:QKV] = pb[:, GW:QKV].astype(BF16)
    pa_ref[...] = _dot(h, w_ref[:, 0:PA_COLS])
    c = c_ref[...]
    s = s_ref[...]
    qkv = jnp.concatenate([_rope(pd[:, 0:GW], c, s) * QK_SCALE, _rope(pd[:, GW:2 * GW], c, s),
                           pd[:, 2 * GW:QKV]], axis=1)
    d1_ref[...] = qkv.astype(BF16)
    for t in range(QKV // LANES):
        lanes = slice(t * LANES, (t + 1) * LANES)
        rope_ref[t] = qkv[:, lanes]
        for d, ref in ((4, d4_ref), (16, d16_ref)):
            for r in range(d):
                ref[r, :, lanes] = rope_ref[t, pl.ds(r, TM_IN // d, stride=d), :].astype(BF16)


def _in_proj(x, w, l, cos_t, sin_t, batch, seq):
    n = x.shape[0]
    nl = seq // TM_IN
    row = lambda cols: pl.BlockSpec((TM_IN, cols), lambda i: (i, 0))
    tab = pl.BlockSpec((TM_IN, GW), lambda i: (i % nl, 0))
    res = lambda d: pl.BlockSpec((None, d, TM_IN // d, QKV), lambda i: (i // nl, 0, i % nl, 0))
    return pl.pallas_call(
        _in_proj_kernel,
        grid=(n // TM_IN,),
        in_specs=[row(D_MODEL),
                  pl.BlockSpec((None, 1, D_MODEL), lambda i: (l, 0, 0)),
                  pl.BlockSpec((None, D_MODEL, P_COLS), lambda i: (l, 0, 0)),
                  tab, tab],
        out_specs=[row(PA_COLS), row(QKV), row(QKV), res(4), res(16)],
        out_shape=[jax.ShapeDtypeStruct((n, PA_COLS), F32),
                   jax.ShapeDtypeStruct((n, QKV), BF16),
                   jax.ShapeDtypeStruct((n, QKV), BF16),
                   jax.ShapeDtypeStruct((batch, 4, seq // 4, QKV), BF16),
                   jax.ShapeDtypeStruct((batch, 16, seq // 16, QKV), BF16)],
        scratch_shapes=[pltpu.VMEM((QKV // LANES, TM_IN, LANES), F32)],
        compiler_params=_params(("parallel",)),
        name="in_proj",
    )(x, w["mix_pre"], w["w_in"], cos_t, sin_t)


TM_FFN = 512
MXU_TILE = 256
FF_CHUNKS = ((0, 6 * MXU_TILE), (6 * MXU_TILE, D_FF))


def _tail_kernel(ya_ref, yb_ref, yc_ref, yd_ref, wm_ref, x_ref, gm_ref, g1_ref, wi_ref, wo_ref, g2_ref, o_ref):
    y = _dot(ya_ref[...].astype(BF16), wm_ref[0 * GW:1 * GW, :])
    y += _dot(yb_ref[...].astype(BF16), wm_ref[1 * GW:2 * GW, :])
    y += _dot(yc_ref[...].astype(BF16), wm_ref[2 * GW:3 * GW, :])
    y += _dot(yd_ref[...].astype(BF16), wm_ref[3 * GW:4 * GW, :])
    x1 = x_ref[...] + _rms(y, gm_ref[...])
    h = _rms(x1, g1_ref[...]).astype(BF16)
    f = None
    for lo, hi in FF_CHUNKS:
        gate = _dot(h, wi_ref[:, lo:hi])
        up = _dot(h, wi_ref[:, D_FF + lo:D_FF + hi])
        part = _dot((gate * _sigmoid(gate) * up).astype(BF16), wo_ref[lo:hi, :])
        f = part if f is None else f + part
    o_ref[...] = x1 + _rms(f, g2_ref[...])


def _tail(ya, yb, yc, yd, x, w, l):
    n = x.shape[0]
    row = pl.BlockSpec((TM_FFN, GW), lambda i: (i, 0))
    gain = pl.BlockSpec((None, 1, D_MODEL), lambda i: (l, 0, 0))
    resident = lambda r, c: pl.BlockSpec((None, r, c), lambda i: (l, 0, 0), pipeline_mode=pl.Buffered(1))
    return pl.pallas_call(
        _tail_kernel,
        grid=(n // TM_FFN,),
        in_specs=[row] * 4 + [
            resident(D_MODEL, D_MODEL),
            pl.BlockSpec((TM_FFN, D_MODEL), lambda i: (i, 0)),
            gain, gain,
            resident(D_MODEL, 2 * D_FF),
            resident(D_FF, D_MODEL),
            gain,
        ],
        out_specs=pl.BlockSpec((TM_FFN, D_MODEL), lambda i: (i, 0)),
        out_shape=jax.ShapeDtypeStruct((n, D_MODEL), F32),
        compiler_params=_params(("parallel",)),
        name="out_proj_ffn",
    )(ya, yb, yc, yd, w["w_out"], x, w["mix_post"], w["ffn_pre"], w["ffn_in"], w["ffn_out"], w["ffn_post"])


TB_GLA = 512


def _gla_kernel(*refs, reverse, finalize):
    if finalize:
        q_ref, k_ref, v_ref, z_ref, wg_ref, bg_ref, of_ref, g_ref, ng_ref, o_ref, st_ref = refs
    else:
        q_ref, k_ref, v_ref, z_ref, wg_ref, bg_ref, o_ref, st_ref = refs
    C = GLA_CHUNK
    n_chunks = TB_GLA // C

    @pl.when(pl.program_id(1) == 0)
    def _():
        st_ref[...] = jnp.zeros_like(st_ref)

    logit = _dot(z_ref[...].astype(BF16), wg_ref[...]) + bg_ref[...]
    log_a = -_softplus(-logit) * (1.0 / GLA_TAU)

    r_i = lax.broadcasted_iota(jnp.int32, (C, C), 0)
    c_i = lax.broadcasted_iota(jnp.int32, (C, C), 1)
    cum = jnp.where((c_i >= r_i) if reverse else (c_i <= r_i), 1.0, 0.0).astype(BF16)
    t_i = lax.broadcasted_iota(jnp.int32, (C, GW), 0)
    s_i = lax.broadcasted_iota(jnp.int32, (C, GW), 1) & (C - 1)
    causal = (s_i >= t_i) if reverse else (s_i <= t_i)
    rb = lax.broadcasted_iota(jnp.int32, (GW, GW), 0) >> 6
    cb = lax.broadcasted_iota(jnp.int32, (GW, GW), 1) >> 6
    same_head = rb == cb
    last, mid = (0, C // 2) if reverse else (C - 1, C // 2 - 1)

    b_c = [_split_dot(cum, log_a[c * C:(c + 1) * C]) for c in range(n_chunks)]
    b = jnp.concatenate(b_c, axis=0)
    b_last = jnp.concatenate([jnp.broadcast_to(t[last:last + 1], (C, GW)) for t in b_c], axis=0)
    b_mid = jnp.concatenate([jnp.broadcast_to(t[mid:mid + 1], (C, GW)) for t in b_c], axis=0)
    q = q_ref[...] * QK_SCALE
    k = k_ref[...]
    v = v_ref[...].astype(BF16)
    q_in = (q * jnp.exp(b - b_mid)).astype(BF16)
    k_in = (k * jnp.exp(b_mid - b)).astype(BF16)
    q_st = (q * jnp.exp(b)).astype(BF16)
    k_st = (k * jnp.exp(b_last - b)).astype(BF16)

    order = range(n_chunks - 1, -1, -1) if reverse else range(n_chunks)
    for c in order:
        rows = slice(c * C, (c + 1) * C)
        v_c = v[rows]
        att = _dot_nt(q_in[rows], _stack_heads(k_in[rows]))
        o = _dot(jnp.where(causal, att, 0.0).astype(BF16), _stack_heads(v_c))
        st = st_ref[...]
        o = o + _dot_nt(q_st[rows], st.astype(BF16))
        upd = _dot_tn(v_c, k_st[rows])
        st_ref[...] = st * jnp.exp(b_c[c][last:last + 1]) + jnp.where(same_head, upd, 0.0)
        o_ref[rows, :] = o

    if finalize:
        o = o_ref[...] + of_ref[...]
        blk = jnp.where(same_head, 1.0 / HEAD_DIM, 0.0).astype(BF16)
        ms = _split_dot_rhs(o * o, blk)
        g = g_ref[...]
        o_ref[...] = o * lax.rsqrt(ms + EPS) * ng_ref[...] * (g * _sigmoid(g))


def _gla_dir(pa, w, l, batch, seq, reverse, o_f=None):
    nb = seq // TB_GLA
    n = batch * seq
    e = 1 if reverse else 0

    def rb(b, i):
        return b * nb + ((nb - 1 - i) if reverse else i)

    col = lambda cb: pl.BlockSpec((TB_GLA, GW), lambda b, i: (rb(b, i), cb))
    in_specs = [col(CB_QA), col(CB_KA), col(CB_VA),
                pl.BlockSpec((TB_GLA, LANES), lambda b, i: (rb(b, i), Z_COL0 // LANES)),
                pl.BlockSpec((None, None, LANES, GW), lambda b, i: (l, e, 0, 0)),
                pl.BlockSpec((None, None, 1, GW), lambda b, i: (l, e, 0, 0))]
    args = [pa, pa, pa, pa, w["gla_wg"], w["gla_bg"]]
    if o_f is not None:
        in_specs += [pl.BlockSpec((TB_GLA, GW), lambda b, i: (rb(b, i), 0)), col(CB_GA),
                     pl.BlockSpec((None, 1, GW), lambda b, i: (l, 0, 0))]
        args += [o_f, pa, w["gla_norm"]]
    return pl.pallas_call(
        functools.partial(_gla_kernel, reverse=reverse, finalize=o_f is not None),
        grid=(batch, nb),
        in_specs=in_specs,
        out_specs=pl.BlockSpec((TB_GLA, GW), lambda b, i: (rb(b, i), 0)),
        out_shape=jax.ShapeDtypeStruct((n, GW), F32),
        scratch_shapes=[pltpu.VMEM((GW, GW), F32)],
        compiler_params=_params(("arbitrary", "arbitrary")),
        name="gla_bwd" if reverse else "gla_fwd",
    )(*args)


NA_BLOCK_ROWS = 8
NA_PAIRS = 2 * NA_ROWS - 2


def _nbr_kernel(q_ref, k_ref, v_ref, bias_ref, o_ref, *, grid_rows):
    W = GRID_W
    win = NA_ROWS * W
    row0 = pl.program_id(1) * NA_BLOCK_ROWS

    def body(j, carry):
        r = row0 + j
        start = jnp.clip(r - NA_ROWS // 2, 0, grid_rows - NA_ROWS)
        pat = start - r + (NA_ROWS - 1)
        k0 = pl.multiple_of(start * W, W)
        q0 = pl.multiple_of(j * W, W)
        s = _dot_nt(_stack_heads(q_ref[pl.ds(q0, W), :]), k_ref[pl.ds(k0, win), :])
        s = s + jnp.concatenate([bias_ref[pat + 2 * t] for t in range(NA_ROWS // 2)], axis=1)
        o, _ = _softmax_pv(s, v_ref[pl.ds(k0, win), :])
        o_ref[pl.ds(q0, W), :] = o
        return carry

    lax.fori_loop(0, NA_BLOCK_ROWS, body, 0, unroll=2)


def _nbr(pb, w, l, batch, seq):
    grid_rows = seq // GRID_W
    tq = NA_BLOCK_ROWS * GRID_W
    nb = seq // tq
    return pl.pallas_call(
        functools.partial(_nbr_kernel, grid_rows=grid_rows),
        grid=(batch, nb),
        in_specs=[
            pl.BlockSpec((tq, GW), lambda b, i: (b * nb + i, 0)),
            pl.BlockSpec((seq, GW), lambda b, i: (b, 1)),
            pl.BlockSpec((seq, GW), lambda b, i: (b, 2)),
            pl.BlockSpec((None, NA_PAIRS, HEADS * GRID_W, 2 * GRID_W), lambda b, i: (l, 0, 0, 0)),
        ],
        out_specs=pl.BlockSpec((tq, GW), lambda b, i: (b * nb + i, 0)),
        out_shape=jax.ShapeDtypeStruct((batch * seq, GW), F32),
        compiler_params=_params(("parallel", "arbitrary")),
        name="nbr_attn",
    )(pb, pb, pb, w["nbr_bias"])


def _nbr_bias(rpb):
    depth = rpb.shape[0]
    c = np.arange(GRID_W)
    dc = np.clip(c[None, :] - c[:, None], -(NA_COLS - 1), NA_COLS - 1) + NA_COLS - 1
    onehot = jnp.asarray(dc.reshape(-1)[None, :] == np.arange(2 * NA_COLS - 1)[:, None], F32)
    t = jnp.dot(rpb.reshape(-1, 2 * NA_COLS - 1), onehot, precision=lax.Precision.HIGHEST)
    t = t.reshape(depth, HEADS, 2 * NA_ROWS - 1, GRID_W, GRID_W)
    col_start = np.clip(c - NA_COLS // 2, 0, GRID_W - NA_COLS)
    col_ok = (c[None, :] >= col_start[:, None]) & (c[None, :] < col_start[:, None] + NA_COLS)
    t = jnp.where(jnp.asarray(col_ok), t, -jnp.inf)
    t = t.transpose(0, 2, 1, 3, 4).reshape(depth, 2 * NA_ROWS - 1, HEADS * GRID_W, GRID_W)
    return jnp.concatenate([t[:, :-1], t[:, 1:]], axis=-1)


TB_LRU = 512
SUB = 8


def _gelu_tanh(x):
    return 0.5 * x * (1.0 + jnp.tanh(0.7978845608028654 * (x + 0.044715 * x * x * x)))


def _lru_kernel(*refs, reverse, finalize):
    if finalize:
        (x_ref, xp_ref, xn_ref, cw_ref, cb_ref, wa_ref, ba_ref, wx_ref, bx_ref, lam_ref,
         hf_ref, gate_ref, o_ref, ext_ref, a_ref, u_ref, carry_ref) = refs
    else:
        (x_ref, xp_ref, xn_ref, cw_ref, cb_ref, wa_ref, ba_ref, wx_ref, bx_ref, lam_ref,
         o_ref, ext_ref, a_ref, u_ref, carry_ref) = refs
    TB = TB_LRU
    i = pl.program_id(1)
    nb = pl.num_programs(1)
    seq_blk = (nb - 1 - i) if reverse else i

    @pl.when(i == 0)
    def _():
        carry_ref[...] = jnp.zeros_like(carry_ref)

    ext_ref[0:SUB, :] = jnp.where(seq_blk == 0, 0.0, xp_ref[...])
    ext_ref[SUB:SUB + TB, :] = x_ref[...]
    ext_ref[SUB + TB:, :] = jnp.where(seq_blk == nb - 1, 0.0, xn_ref[...])
    xc = cb_ref[...] + jnp.zeros((TB, GW), F32)
    for j in range(LRU_CONV):
        xc = xc + ext_ref[pl.ds(SUB - 2 + j, TB), :] * cw_ref[j:j + 1, :]

    xcb = xc.astype(BF16)
    r = _sigmoid(_dot(xcb, wa_ref[...]) + ba_ref[...])
    gi = _sigmoid(_dot(xcb, wx_ref[...]) + bx_ref[...])
    log_a = (-LRU_C) * r * _softplus(-lam_ref[...])
    a = jnp.exp(log_a)
    u = jnp.sqrt(-jnp.tanh(log_a) * (a * a + 1.0)) * (gi * xc)

    a = a.reshape(TB // SUB, SUB, GW)
    u = u.reshape(TB // SUB, SUB, GW)
    rm = lax.broadcasted_iota(jnp.int32, (TB // SUB, SUB, GW), 1)
    for d in (1, 2, 4):
        shift, ok = (SUB - d, rm < SUB - d) if reverse else (d, rm >= d)
        a_sh, u_sh = pltpu.roll(a, shift, 1), pltpu.roll(u, shift, 1)
        u = jnp.where(ok, a * u_sh + u, u)
        a = jnp.where(ok, a * a_sh, a)
    a_ref[...] = a.reshape(TB, GW)
    u_ref[...] = u.reshape(TB, GW)

    nt = TB // SUB
    edge = 0 if reverse else SUB - 1

    def body(t, carry):
        off = pl.multiple_of(((nt - 1 - t) if reverse else t) * SUB, SUB)
        h = u_ref[pl.ds(off, SUB), :] + a_ref[pl.ds(off, SUB), :] * carry
        u_ref[pl.ds(off, SUB), :] = h
        return jnp.broadcast_to(h[edge:edge + 1, :], (SUB, GW))

    carry_ref[...] = lax.fori_loop(0, nt, body, carry_ref[...])

    h = u_ref[...]
    if finalize:
        h = (h + hf_ref[...]) * _gelu_tanh(gate_ref[...])
    o_ref[...] = h


def _lru_dir(pa, w, l, batch, seq, reverse, h_f=None):
    nb = seq // TB_LRU
    n = batch * seq
    tiles = TB_LRU // SUB
    e = 1 if reverse else 0

    def rb(b, i):
        return b * nb + ((nb - 1 - i) if reverse else i)

    per_layer = lambda rows: pl.BlockSpec((None, rows, GW), lambda b, i: (l, 0, 0))
    per_dir = lambda rows: pl.BlockSpec((None, None, rows, GW), lambda b, i: (l, e, 0, 0))
    in_specs = [
        pl.BlockSpec((TB_LRU, GW), lambda b, i: (rb(b, i), CB_XC)),
        pl.BlockSpec((SUB, GW), lambda b, i: (jnp.maximum(rb(b, i) * tiles - 1, 0), CB_XC)),
        pl.BlockSpec((SUB, GW), lambda b, i: (jnp.minimum((rb(b, i) + 1) * tiles, n // SUB - 1), CB_XC)),
        per_layer(LRU_CONV), per_layer(1), per_dir(GW), per_dir(1), per_dir(GW), per_dir(1), per_dir(1),
    ]
    args = [pa, pa, pa, w["lru_conv_w"], w["lru_conv_b"], w["lru_wa"], w["lru_ba"], w["lru_wx"], w["lru_bx"],
            w["lru_lam"]]
    if h_f is not None:
        in_specs += [pl.BlockSpec((TB_LRU, GW), lambda b, i: (rb(b, i), 0)),
                     pl.BlockSpec((TB_LRU, GW), lambda b, i: (rb(b, i), CB_GC))]
        args += [h_f, pa]
    return pl.pallas_call(
        functools.partial(_lru_kernel, reverse=reverse, finalize=h_f is not None),
        grid=(batch, nb),
        in_specs=in_specs,
        out_specs=pl.BlockSpec((TB_LRU, GW), lambda b, i: (rb(b, i), 0)),
        out_shape=jax.ShapeDtypeStruct((n, GW), F32),
        scratch_shapes=[pltpu.VMEM((TB_LRU + 2 * SUB, GW), F32), pltpu.VMEM((TB_LRU, GW), F32),
                        pltpu.VMEM((TB_LRU, GW), F32), pltpu.VMEM((SUB, GW), F32)],
        compiler_params=_params(("arbitrary", "arbitrary")),
        name="lru_bwd" if reverse else "lru_fwd",
    )(*args)


TJ_DIL = 512
SB_DIL = 128
WIN_DIL = SB_DIL + 2 * DIL_RADIUS


def _band_masks():
    q = np.arange(HEADS * SB_DIL)[:, None] % SB_DIL
    c = np.arange(WIN_DIL)[None, :]
    offs = (0, -DIL_RADIUS, -2 * DIL_RADIUS)
    return jnp.asarray(np.stack([np.where(np.abs(c + off - q) <= DIL_RADIUS, 0.0, -np.inf) for off in offs]), F32)


def _dil_kernel(*refs, sub_len, merge):
    if merge:
        (q_ref, k_ref, v_ref, mask_ref, o4_ref, l4_ref, o16_ref, l16_ref, o_ref,
         n4o_ref, n4l_ref, n16o_ref, n16l_ref) = refs
        for d, src_o, src_l, dst_o, dst_l in ((4, o4_ref, l4_ref, n4o_ref, n4l_ref),
                                              (16, o16_ref, l16_ref, n16o_ref, n16l_ref)):
            for r in range(d):
                for t in range(GW // LANES):
                    lanes = slice(t * LANES, (t + 1) * LANES)
                    dst_o[t, pl.ds(r, TJ_DIL // d, stride=d), :] = src_o[r, :, lanes]
                    dst_l[t, pl.ds(r, TJ_DIL // d, stride=d), :] = src_l[r, :, lanes]
    else:
        q_ref, k_ref, v_ref, mask_ref, o_ref, l_ref = refs
    j_blk = pl.program_id(2) * TJ_DIL
    for sb in range(TJ_DIL // SB_DIL):
        rows = slice(sb * SB_DIL, (sb + 1) * SB_DIL)
        j0 = j_blk + sb * SB_DIL
        ws = pl.multiple_of(jnp.clip(j0 - DIL_RADIUS, 0, sub_len - WIN_DIL), DIL_RADIUS)
        which = 1 - (j0 == 0).astype(jnp.int32) + (j0 == sub_len - SB_DIL).astype(jnp.int32)
        s = _dot_nt(_stack_heads(q_ref[rows, :]), k_ref[pl.ds(ws, WIN_DIL), :])
        o, lse = _softmax_pv(s + mask_ref[which], v_ref[pl.ds(ws, WIN_DIL), :])
        if merge:
            tok = lambda ref: jnp.concatenate([ref[t, rows, :] for t in range(GW // LANES)], axis=1)
            l4, l16 = tok(n4l_ref), tok(n16l_ref)
            m = jnp.maximum(jnp.maximum(lse, l4), l16)
            e1, e4, e16 = jnp.exp(lse - m), jnp.exp(l4 - m), jnp.exp(l16 - m)
            o_ref[rows, :] = (e1 * o + e4 * tok(n4o_ref) + e16 * tok(n16o_ref)) / (e1 + e4 + e16)
        else:
            o_ref[rows, :] = o
            l_ref[rows, :] = lse


def _dil_branch(qkv, masks, batch, seq, dil, others=None):
    sub_len = seq // dil
    nj = sub_len // TJ_DIL
    col = lambda cb, rows, jmap: pl.BlockSpec((None, None, rows, GW), lambda b, r, j: (b, r, jmap(j), cb))
    in_specs = [col(0, TJ_DIL, lambda j: j), col(1, sub_len, lambda j: 0), col(2, sub_len, lambda j: 0),
                pl.BlockSpec(masks.shape, lambda b, r, j: (0, 0, 0))]
    out_blk = pl.BlockSpec((None, None, TJ_DIL, GW), lambda b, r, j: (b, r, j, 0))
    out_sd = jax.ShapeDtypeStruct((batch, dil, sub_len, GW), F32)
    args = [qkv, qkv, qkv, masks]
    scratch = []
    if others is None:
        out_specs, out_shape = [out_blk, out_blk], [out_sd, out_sd]
    else:
        assert dil == 1
        for d in (4, 16):
            in_specs += [pl.BlockSpec((None, d, TJ_DIL // d, GW), lambda b, r, j: (b, 0, j, 0))] * 2
        args += list(others)
        out_specs, out_shape = out_blk, out_sd
        scratch = [pltpu.VMEM((GW // LANES, TJ_DIL, LANES), F32)] * 4
    return pl.pallas_call(
        functools.partial(_dil_kernel, sub_len=sub_len, merge=others is not None),
        grid=(batch, dil, nj),
        in_specs=in_specs,
        out_specs=out_specs,
        out_shape=out_shape,
        scratch_shapes=scratch,
        compiler_params=_params(("parallel", "parallel", "arbitrary")),
        name=f"dil_attn_d{dil}",
    )(*args)


def _reorder_w_in(w_in):
    qa = 0
    za = 4 * GW
    qb = za + 2 * GLA_RANK
    xc = qb + QKV
    qd = xc + 2 * GW
    end = qd + QKV
    pad = jnp.zeros(w_in.shape[:-1] + (LANES - 2 * GLA_RANK,), w_in.dtype)
    return jnp.concatenate([w_in[..., qa:za], w_in[..., xc:qd], w_in[..., za:qb], pad,
                            w_in[..., qb:xc], w_in[..., qd:end]], axis=-1).astype(BF16)


def _block_diag(w):
    eye = jnp.eye(HEADS, dtype=w.dtype)[:, None, :, None]
    out = w[..., :, :, None, :] * eye
    return out.reshape(w.shape[:-3] + (GW, GW))


def _gate_weights(w_gate):
    per_dir = [jnp.pad(w_gate[:, e], ((0, 0), (e * GLA_RANK, LANES - (e + 1) * GLA_RANK), (0, 0))) for e in (0, 1)]
    return jnp.stack(per_dir, axis=1).astype(BF16)


def _rope_tables(seq):
    pos = jnp.arange(seq, dtype=F32)
    inv_freq = ROPE_THETA ** (-jnp.arange(0, HEAD_DIM, 2, dtype=F32) / HEAD_DIM)
    ang = pos[:, None] * inv_freq[None, :]
    cos, sin = jnp.cos(ang), jnp.sin(ang)
    return jnp.tile(jnp.concatenate([cos, cos], -1), (1, HEADS)), jnp.tile(jnp.concatenate([-sin, sin], -1), (1, HEADS))


def kernel(x, mix_norm_pre, mix_norm_post, w_in, gla_w_gate, gla_b_gate, gla_norm, na_rpb, lru_conv_w, lru_conv_b, lru_w_a, lru_b_a, lru_w_x, lru_b_x, lru_lambda, w_out, ffn_norm_pre, ffn_norm_post, ffn_w_in, ffn_w_out):
    batch, seq, d_model = x.shape
    assert d_model == D_MODEL and seq % (16 * TJ_DIL) == 0 and seq % TB_GLA == 0
    depth = w_in.shape[0]
    n = batch * seq
    xf = x.reshape(n, D_MODEL)
    cos_t, sin_t = _rope_tables(seq)
    masks = _band_masks()
    rows = lambda t: t[..., None, :]
    w = dict(
        mix_pre=rows(mix_norm_pre), mix_post=rows(mix_norm_post), w_in=_reorder_w_in(w_in),
        gla_wg=_gate_weights(gla_w_gate), gla_bg=rows(gla_b_gate), gla_norm=rows(gla_norm),
        nbr_bias=_nbr_bias(na_rpb),
        lru_conv_w=lru_conv_w, lru_conv_b=rows(lru_conv_b),
        lru_wa=_block_diag(lru_w_a).astype(BF16), lru_ba=rows(lru_b_a),
        lru_wx=_block_diag(lru_w_x).astype(BF16), lru_bx=rows(lru_b_x), lru_lam=rows(lru_lambda),
        w_out=w_out.astype(BF16), ffn_pre=rows(ffn_norm_pre), ffn_post=rows(ffn_norm_post),
        ffn_in=ffn_w_in.astype(BF16), ffn_out=ffn_w_out.astype(BF16),
    )

    for l in range(depth):
        pa, pb, d1, d4, d16 = _in_proj(xf, w, l, cos_t, sin_t, batch, seq)

        o_f = _gla_dir(pa, w, l, batch, seq, False)
        ya = _gla_dir(pa, w, l, batch, seq, True, o_f=o_f)

        yb = _nbr(pb, w, l, batch, seq)

        h_f = _lru_dir(pa, w, l, batch, seq, False)
        yc = _lru_dir(pa, w, l, batch, seq, True, h_f=h_f)

        o4, l4 = _dil_branch(d4, masks, batch, seq, 4)
        o16, l16 = _dil_branch(d16, masks, batch, seq, 16)
        yd = _dil_branch(d1.reshape(batch, 1, seq, QKV), masks, batch, seq, 1, others=(o4, l4, o16, l16))

        xf = _tail(ya, yb, yc, yd.reshape(n, GW), xf, w, l)
    return xf.reshape(batch, seq, D_MODEL)
```

```python
import functools

import numpy as np
import jax
import jax.numpy as jnp
from jax import lax
from jax.experimental import pallas as pl
from jax.experimental.pallas import tpu as pltpu

F32 = jnp.float32
BF16 = jnp.bfloat16

D_MODEL = 1024
HEAD_DIM = 64
HEADS = 4
GW = HEADS * HEAD_DIM
GLA_RANK = 16
GLA_TAU = 16.0
GLA_CHUNK = 64
GRID_W = 64
NA_ROWS = 8
NA_COLS = 16
LRU_C = 8.0
LRU_CONV = 4
DILATIONS = (1, 4, 16)
DIL_RADIUS = 64
ROPE_THETA = 10000.0
D_FF = 2816
EPS = 1e-6
QK_SCALE = HEAD_DIM ** -0.5
LANES = 128

CB_QA, CB_KA, CB_VA, CB_GA, CB_XC, CB_GC = 0, 1, 2, 3, 4, 5
Z_COL0 = 6 * GW
PA_COLS = Z_COL0 + LANES
QKV = 3 * GW
P_COLS = PA_COLS + 2 * QKV

VMEM_LIMIT = 56 * 1024 * 1024


def _params(sem, vmem=VMEM_LIMIT):
    return pltpu.CompilerParams(dimension_semantics=sem, vmem_limit_bytes=vmem)


def _stack_heads(t):
    rows = t.shape[0]
    lane = lax.broadcasted_iota(jnp.int32, (rows, LANES), 1)
    zero = jnp.zeros((rows, LANES), t.dtype)
    blocks = []
    for h in range(HEADS):
        half = t[:, (h // 2) * LANES:(h // 2 + 1) * LANES]
        half = jnp.where((lane < HEAD_DIM) if h % 2 == 0 else (lane >= HEAD_DIM), half, zero)
        blocks.append(jnp.concatenate([half, zero] if h < 2 else [zero, half], axis=1))
    return jnp.concatenate(blocks, axis=0)


def _unstack_heads(t):
    rows = t.shape[0] // HEADS
    first = lax.broadcasted_iota(jnp.int32, (rows, LANES), 1) < HEAD_DIM
    blk = lambda h, tile: t[h * rows:(h + 1) * rows, tile * LANES:(tile + 1) * LANES]
    return jnp.concatenate([jnp.where(first, blk(0, 0), blk(1, 0)), jnp.where(first, blk(2, 1), blk(3, 1))], axis=1)


def _dot(a, b):
    return jnp.dot(a, b, preferred_element_type=F32)


def _dot_nt(a, b):
    return lax.dot_general(a, b, (((1,), (1,)), ((), ())), preferred_element_type=F32)


def _dot_tn(a, b):
    return lax.dot_general(a, b, (((0,), (0,)), ((), ())), preferred_element_type=F32)


def _split(t):
    hi = t.astype(BF16)
    return hi, (t - hi.astype(F32)).astype(BF16)


def _split_dot(m, t):
    hi, lo = _split(t)
    return _dot(m, hi) + _dot(m, lo)


def _split_dot_rhs(t, m):
    hi, lo = _split(t)
    return _dot(hi, m) + _dot(lo, m)


def _rms(x, g):
    ms = jnp.mean(x * x, axis=-1, keepdims=True)
    return x * lax.rsqrt(ms + EPS) * g


def _sigmoid(x):
    return 1.0 / (1.0 + jnp.exp(-x))


def _softplus(x):
    return jnp.maximum(x, 0.0) + jnp.log1p(jnp.exp(-jnp.abs(x)))


def _softmax_pv(scores, values):
    maxes = [jnp.max(s, axis=-1, keepdims=True) for s in scores]
    exps = [jnp.exp(s - m).astype(BF16) for s, m in zip(scores, maxes)]
    pvs = [_dot(e, jnp.concatenate([v, jnp.ones((v.shape[0], LANES), BF16)], axis=1))
           for e, v in zip(exps, values)]
    out = []
    for pv, m in zip(pvs, maxes):
        den = _unstack_heads(jnp.concatenate([pv[:, GW:], pv[:, GW:]], axis=1))
        o = _unstack_heads(pv[:, :GW]) / den
        lse = _unstack_heads(jnp.broadcast_to(m, (pv.shape[0], GW))) + jnp.log(den)
        out.append((o, lse))
    return out


TM_IN = 512


def _rope(t, c, s):
    lane = lax.broadcasted_iota(jnp.int32, t.shape, 1)
    first_half = (lane & (HEAD_DIM - 1)) < HEAD_DIM // 2
    swapped = jnp.where(first_half, pltpu.roll(t, GW - HEAD_DIM // 2, 1), pltpu.roll(t, HEAD_DIM // 2, 1))
    return t * c + swapped * s


def _in_proj_kernel(x_ref, g_ref, w_ref, c_ref, s_ref, pa_ref, pb_ref, d1_ref, d4_ref, d16_ref, rope_ref):
    h = _rms(x_ref[...], g_ref[...]).astype(BF16)
    pd = _dot(h, w_ref[:, PA_COLS + QKV:P_COLS])
    pb = _dot(h, w_ref[:, PA_COLS:PA_COLS + QKV])
    pb_ref[:, 0:GW] = (pb[:, 0:GW] * QK_SCALE).astype(BF16)
    pb_ref[:, GW:QKV] = pb[:, GW:QKV].astype(BF16)
    pa_ref[...] = _dot(h, w_ref[:, 0:PA_COLS])
    c = c_ref[...]
    s = s_ref[...]
    qkv = jnp.concatenate([_rope(pd[:, 0:GW], c, s) * QK_SCALE, _rope(pd[:, GW:2 * GW], c, s),
                           pd[:, 2 * GW:QKV]], axis=1)
    d1_ref[0] = qkv.astype(BF16)
    for t in range(QKV // LANES):
        lanes = slice(t * LANES, (t + 1) * LANES)
        rope_ref[t] = qkv[:, lanes]
        for d, ref in ((4, d4_ref), (16, d16_ref)):
            for r in range(d):
                ref[r, :, lanes] = rope_ref[t, pl.ds(r, TM_IN // d, stride=d), :].astype(BF16)


def _in_proj(x, w, l, cos_t, sin_t, batch, seq):
    n = x.shape[0]
    nl = seq // TM_IN
    row = lambda cols: pl.BlockSpec((TM_IN, cols), lambda i: (i, 0))
    tab = pl.BlockSpec((TM_IN, GW), lambda i: (i % nl, 0))
    res = lambda d: pl.BlockSpec((None, d, TM_IN // d, QKV), lambda i: (i // nl, 0, i % nl, 0))
    return pl.pallas_call(
        _in_proj_kernel,
        grid=(n // TM_IN,),
        in_specs=[row(D_MODEL),
                  pl.BlockSpec((None, 1, D_MODEL), lambda i: (l, 0, 0)),
                  pl.BlockSpec((None, D_MODEL, P_COLS), lambda i: (l, 0, 0)),
                  tab, tab],
        out_specs=[row(PA_COLS), row(QKV), res(1), res(4), res(16)],
        out_shape=[jax.ShapeDtypeStruct((n, PA_COLS), F32),
                   jax.ShapeDtypeStruct((n, QKV), BF16),
                   jax.ShapeDtypeStruct((batch, 1, seq, QKV), BF16),
                   jax.ShapeDtypeStruct((batch, 4, seq // 4, QKV), BF16),
                   jax.ShapeDtypeStruct((batch, 16, seq // 16, QKV), BF16)],
        scratch_shapes=[pltpu.VMEM((QKV // LANES, TM_IN, LANES), F32)],
        compiler_params=_params(("parallel",)),
        name="in_proj",
    )(x, w["mix_pre"], w["w_in"], cos_t, sin_t)


TM_FFN = 512
MXU_TILE = 256
FF_CHUNKS = ((0, 6 * MXU_TILE), (6 * MXU_TILE, D_FF))


def _tail_kernel(ya_ref, yb_ref, yc_ref, yd_ref, wm_ref, x_ref, gm_ref, g1_ref, wi_ref, wo_ref, g2_ref, o_ref):
    y = _dot(ya_ref[...].astype(BF16), wm_ref[0 * GW:1 * GW, :])
    y += _dot(yb_ref[...].astype(BF16), wm_ref[1 * GW:2 * GW, :])
    y += _dot(yc_ref[...].astype(BF16), wm_ref[2 * GW:3 * GW, :])
    y += _dot(yd_ref[...].astype(BF16), wm_ref[3 * GW:4 * GW, :])
    x1 = x_ref[...] + _rms(y, gm_ref[...])
    h = _rms(x1, g1_ref[...]).astype(BF16)
    f = None
    for lo, hi in FF_CHUNKS:
        gate = _dot(h, wi_ref[:, lo:hi])
        up = _dot(h, wi_ref[:, D_FF + lo:D_FF + hi])
        part = _dot((gate * _sigmoid(gate) * up).astype(BF16), wo_ref[lo:hi, :])
        f = part if f is None else f + part
    o_ref[...] = x1 + _rms(f, g2_ref[...])


def _tail(ya, yb, yc, yd, x, w, l):
    n = x.shape[0]
    row = pl.BlockSpec((TM_FFN, GW), lambda i: (i, 0))
    gain = pl.BlockSpec((None, 1, D_MODEL), lambda i: (l, 0, 0))
    resident = lambda r, c: pl.BlockSpec((None, r, c), lambda i: (l, 0, 0), pipeline_mode=pl.Buffered(1))
    return pl.pallas_call(
        _tail_kernel,
        grid=(n // TM_FFN,),
        in_specs=[row] * 4 + [
            resident(D_MODEL, D_MODEL),
            pl.BlockSpec((TM_FFN, D_MODEL), lambda i: (i, 0)),
            gain, gain,
            resident(D_MODEL, 2 * D_FF),
            resident(D_FF, D_MODEL),
            gain,
        ],
        out_specs=pl.BlockSpec((TM_FFN, D_MODEL), lambda i: (i, 0)),
        out_shape=jax.ShapeDtypeStruct((n, D_MODEL), F32),
        compiler_params=_params(("parallel",)),
        name="out_proj_ffn",
    )(ya, yb, yc, yd, w["w_out"], x, w["mix_post"], w["ffn_pre"], w["ffn_in"], w["ffn_out"], w["ffn_post"])


TB_GLA = 512


def _gla_kernel(*refs, reverse, finalize):
    if finalize:
        q_ref, k_ref, v_ref, z_ref, wg_ref, bg_ref, of_ref, g_ref, ng_ref, o_ref, st_ref = refs
    else:
        q_ref, k_ref, v_ref, z_ref, wg_ref, bg_ref, o_ref, st_ref = refs
    C = GLA_CHUNK
    n_chunks = TB_GLA // C

    @pl.when(pl.program_id(1) == 0)
    def _():
        st_ref[...] = jnp.zeros_like(st_ref)

    logit = _dot(z_ref[...].astype(BF16), wg_ref[...]) + bg_ref[...]
    log_a = -_softplus(-logit) * (1.0 / GLA_TAU)

    r_i = lax.broadcasted_iota(jnp.int32, (C, C), 0)
    c_i = lax.broadcasted_iota(jnp.int32, (C, C), 1)
    cum = jnp.where((c_i >= r_i) if reverse else (c_i <= r_i), 1.0, 0.0).astype(BF16)
    t_i = lax.broadcasted_iota(jnp.int32, (C, LANES), 0)
    s_i = lax.broadcasted_iota(jnp.int32, (C, LANES), 1) & (C - 1)
    causal = (s_i >= t_i) if reverse else (s_i <= t_i)
    rb = lax.broadcasted_iota(jnp.int32, (LANES, LANES), 0) >> 6
    cb = lax.broadcasted_iota(jnp.int32, (LANES, LANES), 1) >> 6
    same_head = rb == cb
    last, mid = (0, C // 2) if reverse else (C - 1, C // 2 - 1)

    def stack2(t):
        lane = lax.broadcasted_iota(jnp.int32, t.shape, 1)
        zero = jnp.zeros_like(t)
        return jnp.concatenate([jnp.where(lane < HEAD_DIM, t, zero), jnp.where(lane >= HEAD_DIM, t, zero)], axis=0)

    b_c = [_split_dot(cum, log_a[c * C:(c + 1) * C]) for c in range(n_chunks)]
    b = jnp.concatenate(b_c, axis=0)
    b_mid = jnp.concatenate([jnp.broadcast_to(t[mid:mid + 1], (C, GW)) for t in b_c], axis=0)
    q = q_ref[...] * QK_SCALE
    k = k_ref[...]
    v = v_ref[...].astype(BF16)
    q_mid = q * jnp.exp(b - b_mid)
    k_mid = k * jnp.exp(b_mid - b)
    q_in = q_mid.astype(BF16)
    k_in = k_mid.astype(BF16)
    q_st = jnp.concatenate([q_mid[c * C:(c + 1) * C] * jnp.exp(t[mid:mid + 1]) for c, t in enumerate(b_c)],
                           axis=0).astype(BF16)
    k_st = jnp.concatenate([k_mid[c * C:(c + 1) * C] * jnp.exp(t[last:last + 1] - t[mid:mid + 1])
                            for c, t in enumerate(b_c)], axis=0).astype(BF16)

    chunk = [slice(c * C, (c + 1) * C) for c in range(n_chunks)]
    order = range(n_chunks - 1, -1, -1) if reverse else range(n_chunks)
    for tile in range(GW // LANES):
        lanes = slice(tile * LANES, (tile + 1) * LANES)
        att = [_dot_nt(q_in[r, lanes], stack2(k_in[r, lanes])) for r in chunk]
        o_in = [_dot(jnp.where(causal, a, 0.0).astype(BF16), stack2(v[r, lanes])) for a, r in zip(att, chunk)]
        upd = [_dot_tn(v[r, lanes], k_st[r, lanes]) for r in chunk]
        st = st_ref[tile]
        for c in order:
            o_ref[chunk[c], lanes] = o_in[c] + _dot_nt(q_st[chunk[c], lanes], st.astype(BF16))
            st = st * jnp.exp(b_c[c][last:last + 1, lanes]) + jnp.where(same_head, upd[c], 0.0)
        st_ref[tile] = st

    if finalize:
        o = o_ref[...] + of_ref[...]
        blk = jnp.where(same_head, 1.0 / HEAD_DIM, 0.0).astype(BF16)
        sq = o * o
        ms = jnp.concatenate([_split_dot_rhs(sq[:, t * LANES:(t + 1) * LANES], blk) for t in range(GW // LANES)],
                             axis=1)
        g = g_ref[...]
        o_ref[...] = o * lax.rsqrt(ms + EPS) * ng_ref[...] * (g * _sigmoid(g))


def _gla_dir(pa, w, l, batch, seq, reverse, o_f=None):
    nb = seq // TB_GLA
    n = batch * seq
    e = 1 if reverse else 0

    def rb(b, i):
        return b * nb + ((nb - 1 - i) if reverse else i)

    col = lambda cb: pl.BlockSpec((TB_GLA, GW), lambda b, i: (rb(b, i), cb))
    in_specs = [col(CB_QA), col(CB_KA), col(CB_VA),
                pl.BlockSpec((TB_GLA, LANES), lambda b, i: (rb(b, i), Z_COL0 // LANES)),
                pl.BlockSpec((None, None, LANES, GW), lambda b, i: (l, e, 0, 0)),
                pl.BlockSpec((None, None, 1, GW), lambda b, i: (l, e, 0, 0))]
    args = [pa, pa, pa, pa, w["gla_wg"], w["gla_bg"]]
    if o_f is not None:
        in_specs += [pl.BlockSpec((TB_GLA, GW), lambda b, i: (rb(b, i), 0)), col(CB_GA),
                     pl.BlockSpec((None, 1, GW), lambda b, i: (l, 0, 0))]
        args += [o_f, pa, w["gla_norm"]]
    return pl.pallas_call(
        functools.partial(_gla_kernel, reverse=reverse, finalize=o_f is not None),
        grid=(batch, nb),
        in_specs=in_specs,
        out_specs=pl.BlockSpec((TB_GLA, GW), lambda b, i: (rb(b, i), 0)),
        out_shape=jax.ShapeDtypeStruct((n, GW), F32),
        scratch_shapes=[pltpu.VMEM((GW // LANES, LANES, LANES), F32)],
        compiler_params=_params(("arbitrary", "arbitrary")),
        name="gla_bwd" if reverse else "gla_fwd",
    )(*args)


NA_BLOCK_ROWS = 8
NA_GROUP = 8
NA_PAIRS = 2 * NA_ROWS - 2


def _nbr_kernel(q_ref, k_ref, v_ref, bias_ref, o_ref, *, grid_rows):
    W = GRID_W
    win = NA_ROWS * W
    row0 = pl.program_id(1) * NA_BLOCK_ROWS

    def body(g, carry):
        scores, starts = [], []
        for j in range(NA_GROUP):
            r = row0 + g * NA_GROUP + j
            start = jnp.clip(r - NA_ROWS // 2, 0, grid_rows - NA_ROWS)
            pat = start - r + (NA_ROWS - 1)
            k0 = pl.multiple_of(start * W, W)
            q0 = pl.multiple_of((g * NA_GROUP + j) * W, W)
            s = _dot_nt(_stack_heads(q_ref[pl.ds(q0, W), :]), k_ref[pl.ds(k0, win), :])
            scores.append(s + jnp.concatenate([bias_ref[pat + 2 * t] for t in range(NA_ROWS // 2)], axis=1))
            starts.append(k0)
        results = _softmax_pv(scores, [v_ref[pl.ds(k0, win), :] for k0 in starts])
        for j, (o, _) in enumerate(results):
            o_ref[pl.ds(pl.multiple_of((g * NA_GROUP + j) * W, W), W), :] = o
        return carry

    lax.fori_loop(0, NA_BLOCK_ROWS // NA_GROUP, body, 0)


def _nbr(pb, w, l, batch, seq):
    grid_rows = seq // GRID_W
    tq = NA_BLOCK_ROWS * GRID_W
    nb = seq // tq
    return pl.pallas_call(
        functools.partial(_nbr_kernel, grid_rows=grid_rows),
        grid=(batch, nb),
        in_specs=[
            pl.BlockSpec((tq, GW), lambda b, i: (b * nb + i, 0)),
            pl.BlockSpec((seq, GW), lambda b, i: (b, 1)),
            pl.BlockSpec((seq, GW), lambda b, i: (b, 2)),
            pl.BlockSpec((None, NA_PAIRS, HEADS * GRID_W, 2 * GRID_W), lambda b, i: (l, 0, 0, 0)),
        ],
        out_specs=pl.BlockSpec((tq, GW), lambda b, i: (b * nb + i, 0)),
        out_shape=jax.ShapeDtypeStruct((batch * seq, GW), F32),
        compiler_params=_params(("parallel", "arbitrary")),
        name="nbr_attn",
    )(pb, pb, pb, w["nbr_bias"])


def _nbr_bias(rpb):
    depth = rpb.shape[0]
    c = np.arange(GRID_W)
    dc = np.clip(c[None, :] - c[:, None], -(NA_COLS - 1), NA_COLS - 1) + NA_COLS - 1
    onehot = jnp.asarray(dc.reshape(-1)[None, :] == np.arange(2 * NA_COLS - 1)[:, None], F32)
    t = jnp.dot(rpb.reshape(-1, 2 * NA_COLS - 1), onehot, precision=lax.Precision.HIGHEST)
    t = t.reshape(depth, HEADS, 2 * NA_ROWS - 1, GRID_W, GRID_W)
    col_start = np.clip(c - NA_COLS // 2, 0, GRID_W - NA_COLS)
    col_ok = (c[None, :] >= col_start[:, None]) & (c[None, :] < col_start[:, None] + NA_COLS)
    t = jnp.where(jnp.asarray(col_ok), t, -jnp.inf)
    t = t.transpose(0, 2, 1, 3, 4).reshape(depth, 2 * NA_ROWS - 1, HEADS * GRID_W, GRID_W)
    return jnp.concatenate([t[:, :-1], t[:, 1:]], axis=-1)


TB_LRU = 512
SUB = 8


def _gelu_tanh(x):
    return 0.5 * x * (1.0 + jnp.tanh(0.7978845608028654 * (x + 0.044715 * x * x * x)))


def _lru_kernel(*refs, reverse, finalize):
    if finalize:
        (x_ref, xp_ref, xn_ref, cw_ref, cb_ref, wa_ref, ba_ref, wx_ref, bx_ref, lam_ref,
         hf_ref, gate_ref, o_ref, ext_ref, a_ref, u_ref, carry_ref) = refs
    else:
        (x_ref, xp_ref, xn_ref, cw_ref, cb_ref, wa_ref, ba_ref, wx_ref, bx_ref, lam_ref,
         o_ref, ext_ref, a_ref, u_ref, carry_ref) = refs
    TB = TB_LRU
    i = pl.program_id(1)
    nb = pl.num_programs(1)
    seq_blk = (nb - 1 - i) if reverse else i

    @pl.when(i == 0)
    def _():
        carry_ref[...] = jnp.zeros_like(carry_ref)

    ext_ref[0:SUB, :] = jnp.where(seq_blk == 0, 0.0, xp_ref[...])
    ext_ref[SUB:SUB + TB, :] = x_ref[...]
    ext_ref[SUB + TB:, :] = jnp.where(seq_blk == nb - 1, 0.0, xn_ref[...])
    xc = cb_ref[...] + jnp.zeros((TB, GW), F32)
    for j in range(LRU_CONV):
        xc = xc + ext_ref[pl.ds(SUB - 2 + j, TB), :] * cw_ref[j:j + 1, :]

    xcb = xc.astype(BF16)
    r = _sigmoid(_dot(xcb, wa_ref[...]) + ba_ref[...])
    gi = _sigmoid(_dot(xcb, wx_ref[...]) + bx_ref[...])
    log_a = (-LRU_C) * r * _softplus(-lam_ref[...])
    a = jnp.exp(log_a)
    u = jnp.sqrt(-jnp.tanh(log_a) * (a * a + 1.0)) * (gi * xc)

    a = a.reshape(TB // SUB, SUB, GW)
    u = u.reshape(TB // SUB, SUB, GW)
    rm = lax.broadcasted_iota(jnp.int32, (TB // SUB, SUB, GW), 1)
    for d in (1, 2, 4):
        shift, ok = (SUB - d, rm < SUB - d) if reverse else (d, rm >= d)
        a_sh, u_sh = pltpu.roll(a, shift, 1), pltpu.roll(u, shift, 1)
        u = jnp.where(ok, a * u_sh + u, u)
        a = jnp.where(ok, a * a_sh, a)
    a_ref[...] = a.reshape(TB, GW)
    u_ref[...] = u.reshape(TB, GW)

    nt = TB // SUB
    edge = 0 if reverse else SUB - 1

    def body(t, carry):
        off = pl.multiple_of(((nt - 1 - t) if reverse else t) * SUB, SUB)
        h = u_ref[pl.ds(off, SUB), :] + a_ref[pl.ds(off, SUB), :] * carry
        u_ref[pl.ds(off, SUB), :] = h
        return jnp.broadcast_to(h[edge:edge + 1, :], (SUB, GW))

    carry_ref[...] = lax.fori_loop(0, nt, body, carry_ref[...])

    h = u_ref[...]
    if finalize:
        h = (h + hf_ref[...]) * _gelu_tanh(gate_ref[...])
    o_ref[...] = h


def _lru_dir(pa, w, l, batch, seq, reverse, h_f=None):
    nb = seq // TB_LRU
    n = batch * seq
    tiles = TB_LRU // SUB
    e = 1 if reverse else 0

    def rb(b, i):
        return b * nb + ((nb - 1 - i) if reverse else i)

    per_layer = lambda rows: pl.BlockSpec((None, rows, GW), lambda b, i: (l, 0, 0))
    per_dir = lambda rows: pl.BlockSpec((None, None, rows, GW), lambda b, i: (l, e, 0, 0))
    in_specs = [
        pl.BlockSpec((TB_LRU, GW), lambda b, i: (rb(b, i), CB_XC)),
        pl.BlockSpec((SUB, GW), lambda b, i: (jnp.maximum(rb(b, i) * tiles - 1, 0), CB_XC)),
        pl.BlockSpec((SUB, GW), lambda b, i: (jnp.minimum((rb(b, i) + 1) * tiles, n // SUB - 1), CB_XC)),
        per_layer(LRU_CONV), per_layer(1), per_dir(GW), per_dir(1), per_dir(GW), per_dir(1), per_dir(1),
    ]
    args = [pa, pa, pa, w["lru_conv_w"], w["lru_conv_b"], w["lru_wa"], w["lru_ba"], w["lru_wx"], w["lru_bx"],
            w["lru_lam"]]
    if h_f is not None:
        in_specs += [pl.BlockSpec((TB_LRU, GW), lambda b, i: (rb(b, i), 0)),
                     pl.BlockSpec((TB_LRU, GW), lambda b, i: (rb(b, i), CB_GC))]
        args += [h_f, pa]
    return pl.pallas_call(
        functools.partial(_lru_kernel, reverse=reverse, finalize=h_f is not None),
        grid=(batch, nb),
        in_specs=in_specs,
        out_specs=pl.BlockSpec((TB_LRU, GW), lambda b, i: (rb(b, i), 0)),
        out_shape=jax.ShapeDtypeStruct((n, GW), F32),
        scratch_shapes=[pltpu.VMEM((TB_LRU + 2 * SUB, GW), F32), pltpu.VMEM((TB_LRU, GW), F32),
                        pltpu.VMEM((TB_LRU, GW), F32), pltpu.VMEM((SUB, GW), F32)],
        compiler_params=_params(("arbitrary", "arbitrary")),
        name="lru_bwd" if reverse else "lru_fwd",
    )(*args)


TJ_DIL = 512
SB_DIL = 128
WIN_DIL = SB_DIL + 2 * DIL_RADIUS


def _band_masks():
    q = np.arange(HEADS * SB_DIL)[:, None] % SB_DIL
    c = np.arange(WIN_DIL)[None, :]
    offs = (0, -DIL_RADIUS, -2 * DIL_RADIUS)
    return jnp.asarray(np.stack([np.where(np.abs(c + off - q) <= DIL_RADIUS, 0.0, -np.inf) for off in offs]), F32)


def _dil_kernel(*refs, sub_len, merge):
    if merge:
        (q_ref, k_ref, v_ref, mask_ref, o4_ref, l4_ref, o16_ref, l16_ref, o_ref,
         n4o_ref, n4l_ref, n16o_ref, n16l_ref) = refs
        for d, src_o, src_l, dst_o, dst_l in ((4, o4_ref, l4_ref, n4o_ref, n4l_ref),
                                              (16, o16_ref, l16_ref, n16o_ref, n16l_ref)):
            for r in range(d):
                for t in range(GW // LANES):
                    lanes = slice(t * LANES, (t + 1) * LANES)
                    dst_o[t, pl.ds(r, TJ_DIL // d, stride=d), :] = src_o[r, :, lanes]
                    dst_l[t, pl.ds(r, TJ_DIL // d, stride=d), :] = src_l[r, :, lanes]
    else:
        q_ref, k_ref, v_ref, mask_ref, o_ref, l_ref = refs
    j_blk = pl.program_id(2) * TJ_DIL
    scores, starts = [], []
    for sb in range(TJ_DIL // SB_DIL):
        rows = slice(sb * SB_DIL, (sb + 1) * SB_DIL)
        j0 = j_blk + sb * SB_DIL
        ws = pl.multiple_of(jnp.clip(j0 - DIL_RADIUS, 0, sub_len - WIN_DIL), DIL_RADIUS)
        which = 1 - (j0 == 0).astype(jnp.int32) + (j0 == sub_len - SB_DIL).astype(jnp.int32)
        s = _dot_nt(_stack_heads(q_ref[rows, :]), k_ref[pl.ds(ws, WIN_DIL), :])
        scores.append(s + mask_ref[which])
        starts.append(ws)
    results = _softmax_pv(scores, [v_ref[pl.ds(ws, WIN_DIL), :] for ws in starts])
    for sb, (o, lse) in enumerate(results):
        rows = slice(sb * SB_DIL, (sb + 1) * SB_DIL)
        if merge:
            tok = lambda ref: jnp.concatenate([ref[t, rows, :] for t in range(GW // LANES)], axis=1)
            l4, l16 = tok(n4l_ref), tok(n16l_ref)
            m = jnp.maximum(jnp.maximum(lse, l4), l16)
            e1, e4, e16 = jnp.exp(lse - m), jnp.exp(l4 - m), jnp.exp(l16 - m)
            o_ref[rows, :] = (e1 * o + e4 * tok(n4o_ref) + e16 * tok(n16o_ref)) / (e1 + e4 + e16)
        else:
            o_ref[rows, :] = o
            l_ref[rows, :] = lse


def _dil_branch(qkv, masks, batch, seq, dil, others=None):
    sub_len = seq // dil
    nj = sub_len // TJ_DIL
    col = lambda cb, rows, jmap: pl.BlockSpec((None, None, rows, GW), lambda b, r, j: (b, r, jmap(j), cb))
    in_specs = [col(0, TJ_DIL, lambda j: j), col(1, sub_len, lambda j: 0), col(2, sub_len, lambda j: 0),
                pl.BlockSpec(masks.shape, lambda b, r, j: (0, 0, 0))]
    out_blk = pl.BlockSpec((None, None, TJ_DIL, GW), lambda b, r, j: (b, r, j, 0))
    out_sd = jax.ShapeDtypeStruct((batch, dil, sub_len, GW), F32)
    args = [qkv, qkv, qkv, masks]
    scratch = []
    if others is None:
        out_specs, out_shape = [out_blk, out_blk], [out_sd, out_sd]
    else:
        assert dil == 1
        for d in (4, 16):
            in_specs += [pl.BlockSpec((None, d, TJ_DIL // d, GW), lambda b, r, j: (b, 0, j, 0))] * 2
        args += list(others)
        out_specs = pl.BlockSpec((TJ_DIL, GW), lambda b, r, j: (b * nj + j, 0))
        out_shape = jax.ShapeDtypeStruct((batch * seq, GW), F32)
        scratch = [pltpu.VMEM((GW // LANES, TJ_DIL, LANES), F32)] * 4
    return pl.pallas_call(
        functools.partial(_dil_kernel, sub_len=sub_len, merge=others is not None),
        grid=(batch, dil, nj),
        in_specs=in_specs,
        out_specs=out_specs,
        out_shape=out_shape,
        scratch_shapes=scratch,
        compiler_params=_params(("parallel", "parallel", "arbitrary")),
        name=f"dil_attn_d{dil}",
    )(*args)


def _reorder_w_in(w_in):
    qa = 0
    za = 4 * GW
    qb = za + 2 * GLA_RANK
    xc = qb + QKV
    qd = xc + 2 * GW
    end = qd + QKV
    pad = jnp.zeros(w_in.shape[:-1] + (LANES - 2 * GLA_RANK,), w_in.dtype)
    return jnp.concatenate([w_in[..., qa:za], w_in[..., xc:qd], w_in[..., za:qb], pad,
                            w_in[..., qb:xc], w_in[..., qd:end]], axis=-1).astype(BF16)


def _block_diag(w):
    eye = jnp.eye(HEADS, dtype=w.dtype)[:, None, :, None]
    out = w[..., :, :, None, :] * eye
    return out.reshape(w.shape[:-3] + (GW, GW))


def _gate_weights(w_gate):
    per_dir = [jnp.pad(w_gate[:, e], ((0, 0), (e * GLA_RANK, LANES - (e + 1) * GLA_RANK), (0, 0))) for e in (0, 1)]
    return jnp.stack(per_dir, axis=1).astype(BF16)


def _rope_tables(seq):
    pos = jnp.arange(seq, dtype=F32)
    inv_freq = ROPE_THETA ** (-jnp.arange(0, HEAD_DIM, 2, dtype=F32) / HEAD_DIM)
    ang = pos[:, None] * inv_freq[None, :]
    cos, sin = jnp.cos(ang), jnp.sin(ang)
    return jnp.tile(jnp.concatenate([cos, cos], -1), (1, HEADS)), jnp.tile(jnp.concatenate([-sin, sin], -1), (1, HEADS))


def kernel(x, mix_norm_pre, mix_norm_post, w_in, gla_w_gate, gla_b_gate, gla_norm, na_rpb, lru_conv_w, lru_conv_b, lru_w_a, lru_b_a, lru_w_x, lru_b_x, lru_lambda, w_out, ffn_norm_pre, ffn_norm_post, ffn_w_in, ffn_w_out):
    batch, seq, d_model = x.shape
    assert d_model == D_MODEL and seq % (16 * TJ_DIL) == 0 and seq % TB_GLA == 0
    depth = w_in.shape[0]
    n = batch * seq
    xf = x.reshape(n, D_MODEL)
    cos_t, sin_t = _rope_tables(seq)
    masks = _band_masks()
    rows = lambda t: t[..., None, :]
    w = dict(
        mix_pre=rows(mix_norm_pre), mix_post=rows(mix_norm_post), w_in=_reorder_w_in(w_in),
        gla_wg=_gate_weights(gla_w_gate), gla_bg=rows(gla_b_gate), gla_norm=rows(gla_norm),
        nbr_bias=_nbr_bias(na_rpb),
        lru_conv_w=lru_conv_w, lru_conv_b=rows(lru_conv_b),
        lru_wa=_block_diag(lru_w_a).astype(BF16), lru_ba=rows(lru_b_a),
        lru_wx=_block_diag(lru_w_x).astype(BF16), lru_bx=rows(lru_b_x), lru_lam=rows(lru_lambda),
        w_out=w_out.astype(BF16), ffn_pre=rows(ffn_norm_pre), ffn_post=rows(ffn_norm_post),
        ffn_in=ffn_w_in.astype(BF16), ffn_out=ffn_w_out.astype(BF16),
    )

    for l in range(depth):
        pa, pb, d1, d4, d16 = _in_proj(xf, w, l, cos_t, sin_t, batch, seq)

        o_f = _gla_dir(pa, w, l, batch, seq, False)
        ya = _gla_dir(pa, w, l, batch, seq, True, o_f=o_f)

        yb = _nbr(pb, w, l, batch, seq)

        h_f = _lru_dir(pa, w, l, batch, seq, False)
        yc = _lru_dir(pa, w, l, batch, seq, True, h_f=h_f)

        o4, l4 = _dil_branch(d4, masks, batch, seq, 4)
        o16, l16 = _dil_branch(d16, masks, batch, seq, 16)
        yd = _dil_branch(d1, masks, batch, seq, 1, others=(o4, l4, o16, l16))

        xf = _tail(ya, yb, yc, yd, xf, w, l)
    return xf.reshape(batch, seq, D_MODEL)
```

```python
import functools

import numpy as np
import jax
import jax.numpy as jnp
from jax import lax
from jax.experimental import pallas as pl
from jax.experimental.pallas import tpu as pltpu

F32 = jnp.float32
BF16 = jnp.bfloat16

D_MODEL = 1024
HEAD_DIM = 64
HEADS = 4
GW = HEADS * HEAD_DIM
GLA_RANK = 16
GLA_TAU = 16.0
GLA_CHUNK = 64
GRID_W = 64
NA_ROWS = 8
NA_COLS = 16
LRU_C = 8.0
LRU_CONV = 4
DILATIONS = (1, 4, 16)
DIL_RADIUS = 64
ROPE_THETA = 10000.0
D_FF = 2816
EPS = 1e-6
QK_SCALE = HEAD_DIM ** -0.5
LANES = 128

CB_QA, CB_KA, CB_VA, CB_GA, CB_XC, CB_GC = 0, 1, 2, 3, 4, 5
Z_COL0 = 6 * GW
PA_COLS = Z_COL0 + LANES
QKV = 3 * GW
P_COLS = PA_COLS + 2 * QKV

VMEM_LIMIT = 56 * 1024 * 1024


def _params(sem, vmem=VMEM_LIMIT):
    return pltpu.CompilerParams(dimension_semantics=sem, vmem_limit_bytes=vmem)


def _stack_heads(t):
    rows = t.shape[0]
    lane = lax.broadcasted_iota(jnp.int32, (rows, LANES), 1)
    zero = jnp.zeros((rows, LANES), t.dtype)
    blocks = []
    for h in range(HEADS):
        half = t[:, (h // 2) * LANES:(h // 2 + 1) * LANES]
        half = jnp.where((lane < HEAD_DIM) if h % 2 == 0 else (lane >= HEAD_DIM), half, zero)
        blocks.append(jnp.concatenate([half, zero] if h < 2 else [zero, half], axis=1))
    return jnp.concatenate(blocks, axis=0)


def _unstack_heads(t):
    rows = t.shape[0] // HEADS
    first = lax.broadcasted_iota(jnp.int32, (rows, LANES), 1) < HEAD_DIM
    blk = lambda h, tile: t[h * rows:(h + 1) * rows, tile * LANES:(tile + 1) * LANES]
    return jnp.concatenate([jnp.where(first, blk(0, 0), blk(1, 0)), jnp.where(first, blk(2, 1), blk(3, 1))], axis=1)


def _dot(a, b):
    return jnp.dot(a, b, preferred_element_type=F32)


def _dot_nt(a, b):
    return lax.dot_general(a, b, (((1,), (1,)), ((), ())), preferred_element_type=F32)


def _dot_tn(a, b):
    return lax.dot_general(a, b, (((0,), (0,)), ((), ())), preferred_element_type=F32)


def _split(t):
    hi = t.astype(BF16)
    return hi, (t - hi.astype(F32)).astype(BF16)


def _split_dot(m, t):
    hi, lo = _split(t)
    return _dot(m, hi) + _dot(m, lo)


def _split_dot_rhs(t, m):
    hi, lo = _split(t)
    return _dot(hi, m) + _dot(lo, m)


def _rms(x, g):
    ms = jnp.mean(x * x, axis=-1, keepdims=True)
    return x * lax.rsqrt(ms + EPS) * g


def _sigmoid(x):
    return 1.0 / (1.0 + jnp.exp(-x))


def _softplus(x):
    return jnp.maximum(x, 0.0) + jnp.log1p(jnp.exp(-jnp.abs(x)))


def _softmax_pv(scores, values):
    maxes = [jnp.max(s, axis=-1, keepdims=True) for s in scores]
    exps = [jnp.exp(s - m) for s, m in zip(scores, maxes)]
    sums = [jnp.sum(e, axis=-1, keepdims=True) for e in exps]
    pvs = [_dot(e.astype(BF16), v) for e, v in zip(exps, values)]
    out = []
    for pv, m, d in zip(pvs, maxes, sums):
        den = _unstack_heads(jnp.broadcast_to(d, pv.shape))
        o = _unstack_heads(pv) / den
        lse = _unstack_heads(jnp.broadcast_to(m, pv.shape)) + jnp.log(den)
        out.append((o, lse))
    return out


TM_IN = 512


def _rope(t, c, s):
    lane = lax.broadcasted_iota(jnp.int32, t.shape, 1)
    first_half = (lane & (HEAD_DIM - 1)) < HEAD_DIM // 2
    swapped = jnp.where(first_half, pltpu.roll(t, GW - HEAD_DIM // 2, 1), pltpu.roll(t, HEAD_DIM // 2, 1))
    return t * c + swapped * s


def _in_proj_kernel(x_ref, g_ref, w_ref, c_ref, s_ref, pa_ref, pb_ref, d1_ref, d4_ref, d16_ref, rope_ref):
    h = _rms(x_ref[...], g_ref[...]).astype(BF16)
    pd = _dot(h, w_ref[:, PA_COLS + QKV:P_COLS])
    pb = _dot(h, w_ref[:, PA_COLS:PA_COLS + QKV])
    pb_ref[:, 0:GW] = (pb[:, 0:GW] * QK_SCALE).astype(BF16)
    pb_ref[:, GW:QKV] = pb[:, GW:QKV].astype(BF16)
    pa_ref[...] = _dot(h, w_ref[:, 0:PA_COLS])
    c = c_ref[...]
    s = s_ref[...]
    qkv = jnp.concatenate([_rope(pd[:, 0:GW], c, s) * QK_SCALE, _rope(pd[:, GW:2 * GW], c, s),
                           pd[:, 2 * GW:QKV]], axis=1)
    d1_ref[0] = qkv.astype(BF16)
    for t in range(QKV // LANES):
        lanes = slice(t * LANES, (t + 1) * LANES)
        rope_ref[t] = qkv[:, lanes]
        for d, ref in ((4, d4_ref), (16, d16_ref)):
            for r in range(d):
                ref[r, :, lanes] = rope_ref[t, pl.ds(r, TM_IN // d, stride=d), :].astype(BF16)


def _in_proj(x, w, l, cos_t, sin_t, batch, seq):
    n = x.shape[0]
    nl = seq // TM_IN
    row = lambda cols: pl.BlockSpec((TM_IN, cols), lambda i: (i, 0))
    tab = pl.BlockSpec((TM_IN, GW), lambda i: (i % nl, 0))
    res = lambda d: pl.BlockSpec((None, d, TM_IN // d, QKV), lambda i: (i // nl, 0, i % nl, 0))
    return pl.pallas_call(
        _in_proj_kernel,
        grid=(n // TM_IN,),
        in_specs=[row(D_MODEL),
                  pl.BlockSpec((None, 1, D_MODEL), lambda i: (l, 0, 0)),
                  pl.BlockSpec((None, D_MODEL, P_COLS), lambda i: (l, 0, 0)),
                  tab, tab],
        out_specs=[row(PA_COLS), row(QKV), res(1), res(4), res(16)],
        out_shape=[jax.ShapeDtypeStruct((n, PA_COLS), F32),
                   jax.ShapeDtypeStruct((n, QKV), BF16),
                   jax.ShapeDtypeStruct((batch, 1, seq, QKV), BF16),
                   jax.ShapeDtypeStruct((batch, 4, seq // 4, QKV), BF16),
                   jax.ShapeDtypeStruct((batch, 16, seq // 16, QKV), BF16)],
        scratch_shapes=[pltpu.VMEM((QKV // LANES, TM_IN, LANES), F32)],
        compiler_params=_params(("parallel",)),
        name="in_proj",
    )(x, w["mix_pre"], w["w_in"], cos_t, sin_t)


TM_FFN = 512
MXU_TILE = 256
FF_CHUNKS = ((0, 6 * MXU_TILE), (6 * MXU_TILE, D_FF))


def _tail_kernel(af_ref, ab_ref, ag_ref, an_ref, yb_ref, cf_ref, cb_ref, cg_ref, yd_ref,
                 wm_ref, x_ref, gm_ref, g1_ref, wi_ref, wo_ref, g2_ref, o_ref):
    ya = _gla_finalize(af_ref[...] + ab_ref[...], ag_ref[...], an_ref[...])
    yc = (cf_ref[...] + cb_ref[...]) * _gelu_tanh(cg_ref[...])
    y = _dot(ya.astype(BF16), wm_ref[0 * GW:1 * GW, :])
    y += _dot(yb_ref[...].astype(BF16), wm_ref[1 * GW:2 * GW, :])
    y += _dot(yc.astype(BF16), wm_ref[2 * GW:3 * GW, :])
    y += _dot(yd_ref[...].astype(BF16), wm_ref[3 * GW:4 * GW, :])
    x1 = x_ref[...] + _rms(y, gm_ref[...])
    h = _rms(x1, g1_ref[...]).astype(BF16)
    f = None
    for lo, hi in FF_CHUNKS:
        gate = _dot(h, wi_ref[:, lo:hi])
        up = _dot(h, wi_ref[:, D_FF + lo:D_FF + hi])
        part = _dot((gate * _sigmoid(gate) * up).astype(BF16), wo_ref[lo:hi, :])
        f = part if f is None else f + part
    o_ref[...] = x1 + _rms(f, g2_ref[...])


def _tail(pa, gla_dirs, yb, lru_dirs, yd, x, w, l):
    n = x.shape[0]
    row = pl.BlockSpec((TM_FFN, GW), lambda i: (i, 0))
    stream = lambda cb: pl.BlockSpec((TM_FFN, GW), lambda i: (i, cb))
    gain = pl.BlockSpec((None, 1, D_MODEL), lambda i: (l, 0, 0))
    resident = lambda r, c: pl.BlockSpec((None, r, c), lambda i: (l, 0, 0), pipeline_mode=pl.Buffered(1))
    return pl.pallas_call(
        _tail_kernel,
        grid=(n // TM_FFN,),
        in_specs=[row, row, stream(CB_GA), pl.BlockSpec((None, 1, GW), lambda i: (l, 0, 0)), row,
                  row, row, stream(CB_GC), row] + [
            resident(D_MODEL, D_MODEL),
            pl.BlockSpec((TM_FFN, D_MODEL), lambda i: (i, 0)),
            gain, gain,
            resident(D_MODEL, 2 * D_FF),
            resident(D_FF, D_MODEL),
            gain,
        ],
        out_specs=pl.BlockSpec((TM_FFN, D_MODEL), lambda i: (i, 0)),
        out_shape=jax.ShapeDtypeStruct((n, D_MODEL), F32),
        compiler_params=_params(("parallel",)),
        name="out_proj_ffn",
    )(gla_dirs[0], gla_dirs[1], pa, w["gla_norm"], yb, lru_dirs[0], lru_dirs[1], pa, yd,
      w["w_out"], x, w["mix_post"], w["ffn_pre"], w["ffn_in"], w["ffn_out"], w["ffn_post"])


TB_GLA = 512


def _same_head_tile():
    rb = lax.broadcasted_iota(jnp.int32, (LANES, LANES), 0) >> 6
    cb = lax.broadcasted_iota(jnp.int32, (LANES, LANES), 1) >> 6
    return rb == cb


def _gla_direction(q_ref, k_ref, v_ref, z_ref, wg_ref, bg_ref, o_ref, st_ref, reverse):
    C = GLA_CHUNK
    n_chunks = TB_GLA // C
    logit = _dot(z_ref[...].astype(BF16), wg_ref[...]) + bg_ref[...]
    log_a = -_softplus(-logit) * (1.0 / GLA_TAU)

    r_i = lax.broadcasted_iota(jnp.int32, (C, C), 0)
    c_i = lax.broadcasted_iota(jnp.int32, (C, C), 1)
    cum = jnp.where((c_i >= r_i) if reverse else (c_i <= r_i), 1.0, 0.0).astype(BF16)
    t_i = lax.broadcasted_iota(jnp.int32, (C, LANES), 0)
    s_i = lax.broadcasted_iota(jnp.int32, (C, LANES), 1) & (C - 1)
    causal = (s_i >= t_i) if reverse else (s_i <= t_i)
    same_head = _same_head_tile()
    last, mid = (0, C // 2) if reverse else (C - 1, C // 2 - 1)

    def stack2(t):
        lane = lax.broadcasted_iota(jnp.int32, t.shape, 1)
        zero = jnp.zeros_like(t)
        return jnp.concatenate([jnp.where(lane < HEAD_DIM, t, zero), jnp.where(lane >= HEAD_DIM, t, zero)], axis=0)

    b_c = [_split_dot(cum, log_a[c * C:(c + 1) * C]) for c in range(n_chunks)]
    b = jnp.concatenate(b_c, axis=0)
    b_mid = jnp.concatenate([jnp.broadcast_to(t[mid:mid + 1], (C, GW)) for t in b_c], axis=0)
    q = q_ref[...] * QK_SCALE
    k = k_ref[...]
    v = v_ref[...].astype(BF16)
    q_mid = q * jnp.exp(b - b_mid)
    k_mid = k * jnp.exp(b_mid - b)
    q_in = q_mid.astype(BF16)
    k_in = k_mid.astype(BF16)
    q_st = jnp.concatenate([q_mid[c * C:(c + 1) * C] * jnp.exp(t[mid:mid + 1]) for c, t in enumerate(b_c)],
                           axis=0).astype(BF16)
    k_st = jnp.concatenate([k_mid[c * C:(c + 1) * C] * jnp.exp(t[last:last + 1] - t[mid:mid + 1])
                            for c, t in enumerate(b_c)], axis=0).astype(BF16)

    yield

    chunk = [slice(c * C, (c + 1) * C) for c in range(n_chunks)]
    order = range(n_chunks - 1, -1, -1) if reverse else range(n_chunks)
    for tile in range(GW // LANES):
        lanes = slice(tile * LANES, (tile + 1) * LANES)
        att = [_dot_nt(q_in[r, lanes], stack2(k_in[r, lanes])) for r in chunk]
        yield
        o_in = [_dot(jnp.where(causal, a, 0.0).astype(BF16), stack2(v[r, lanes])) for a, r in zip(att, chunk)]
        upd = [_dot_tn(v[r, lanes], k_st[r, lanes]) for r in chunk]
        yield
        st = st_ref[tile]
        for c in order:
            o_ref[chunk[c], lanes] = o_in[c] + _dot_nt(q_st[chunk[c], lanes], st.astype(BF16))
            st = st * jnp.exp(b_c[c][last:last + 1, lanes]) + jnp.where(same_head, upd[c], 0.0)
            yield
        st_ref[tile] = st


def _gla_kernel(qf_ref, kf_ref, vf_ref, zf_ref, qb_ref, kb_ref, vb_ref, zb_ref, wg_ref, bg_ref,
                of_ref, ob_ref, stf_ref, stb_ref):
    @pl.when(pl.program_id(1) == 0)
    def _():
        stf_ref[...] = jnp.zeros_like(stf_ref)
        stb_ref[...] = jnp.zeros_like(stb_ref)

    stages = [_gla_direction(qf_ref, kf_ref, vf_ref, zf_ref, wg_ref.at[0], bg_ref.at[0], of_ref, stf_ref, False),
              _gla_direction(qb_ref, kb_ref, vb_ref, zb_ref, wg_ref.at[1], bg_ref.at[1], ob_ref, stb_ref, True)]
    while stages:
        stages = [s for s in stages if next(s, True) is None]


def _gla_finalize(o, g, norm_g):
    blk = jnp.where(_same_head_tile(), 1.0 / HEAD_DIM, 0.0).astype(BF16)
    sq = o * o
    ms = jnp.concatenate([_split_dot_rhs(sq[:, t * LANES:(t + 1) * LANES], blk) for t in range(GW // LANES)],
                         axis=1)
    return o * lax.rsqrt(ms + EPS) * norm_g * (g * _sigmoid(g))


def _gla(pa, w, l, batch, seq):
    nb = seq // TB_GLA
    n = batch * seq
    fwd = lambda b, i: b * nb + i
    bwd = lambda b, i: b * nb + nb - 1 - i
    col = lambda rb, cb: pl.BlockSpec((TB_GLA, GW), lambda b, i: (rb(b, i), cb))
    gate = lambda rb: pl.BlockSpec((TB_GLA, LANES), lambda b, i: (rb(b, i), Z_COL0 // LANES))
    out_sd = jax.ShapeDtypeStruct((n, GW), F32)
    state = pltpu.VMEM((GW // LANES, LANES, LANES), F32)
    return pl.pallas_call(
        _gla_kernel,
        grid=(batch, nb),
        in_specs=[col(fwd, CB_QA), col(fwd, CB_KA), col(fwd, CB_VA), gate(fwd),
                  col(bwd, CB_QA), col(bwd, CB_KA), col(bwd, CB_VA), gate(bwd),
                  pl.BlockSpec((None, 2, LANES, GW), lambda b, i: (l, 0, 0, 0)),
                  pl.BlockSpec((None, 2, 1, GW), lambda b, i: (l, 0, 0, 0))],
        out_specs=[col(fwd, 0), col(bwd, 0)],
        out_shape=[out_sd, out_sd],
        scratch_shapes=[state, state],
        compiler_params=_params(("arbitrary", "arbitrary")),
        name="gla",
    )(pa, pa, pa, pa, pa, pa, pa, pa, w["gla_wg"], w["gla_bg"])


NA_BLOCK_ROWS = 8
NA_GROUP = 8
NA_PAIRS = 2 * NA_ROWS - 2


def _nbr_kernel(q_ref, k_ref, v_ref, bias_ref, o_ref, *, grid_rows):
    W = GRID_W
    win = NA_ROWS * W
    row0 = pl.program_id(1) * NA_BLOCK_ROWS

    def body(g, carry):
        scores, starts = [], []
        for j in range(NA_GROUP):
            r = row0 + g * NA_GROUP + j
            start = jnp.clip(r - NA_ROWS // 2, 0, grid_rows - NA_ROWS)
            pat = start - r + (NA_ROWS - 1)
            k0 = pl.multiple_of(start * W, W)
            q0 = pl.multiple_of((g * NA_GROUP + j) * W, W)
            s = _dot_nt(_stack_heads(q_ref[pl.ds(q0, W), :]), k_ref[pl.ds(k0, win), :])
            scores.append(s + jnp.concatenate([bias_ref[pat + 2 * t] for t in range(NA_ROWS // 2)], axis=1))
            starts.append(k0)
        results = _softmax_pv(scores, [v_ref[pl.ds(k0, win), :] for k0 in starts])
        for j, (o, _) in enumerate(results):
            o_ref[pl.ds(pl.multiple_of((g * NA_GROUP + j) * W, W), W), :] = o
        return carry

    lax.fori_loop(0, NA_BLOCK_ROWS // NA_GROUP, body, 0)


def _nbr(pb, w, l, batch, seq):
    grid_rows = seq // GRID_W
    tq = NA_BLOCK_ROWS * GRID_W
    nb = seq // tq
    return pl.pallas_call(
        functools.partial(_nbr_kernel, grid_rows=grid_rows),
        grid=(batch, nb),
        in_specs=[
            pl.BlockSpec((tq, GW), lambda b, i: (b * nb + i, 0)),
            pl.BlockSpec((seq, GW), lambda b, i: (b, 1)),
            pl.BlockSpec((seq, GW), lambda b, i: (b, 2)),
            pl.BlockSpec((None, NA_PAIRS, HEADS * GRID_W, 2 * GRID_W), lambda b, i: (l, 0, 0, 0)),
        ],
        out_specs=pl.BlockSpec((tq, GW), lambda b, i: (b * nb + i, 0)),
        out_shape=jax.ShapeDtypeStruct((batch * seq, GW), F32),
        compiler_params=_params(("parallel", "arbitrary")),
        name="nbr_attn",
    )(pb, pb, pb, w["nbr_bias"])


def _nbr_bias(rpb):
    depth = rpb.shape[0]
    c = np.arange(GRID_W)
    dc = np.clip(c[None, :] - c[:, None], -(NA_COLS - 1), NA_COLS - 1) + NA_COLS - 1
    onehot = jnp.asarray(dc.reshape(-1)[None, :] == np.arange(2 * NA_COLS - 1)[:, None], F32)
    t = jnp.dot(rpb.reshape(-1, 2 * NA_COLS - 1), onehot, precision=lax.Precision.HIGHEST)
    t = t.reshape(depth, HEADS, 2 * NA_ROWS - 1, GRID_W, GRID_W)
    col_start = np.clip(c - NA_COLS // 2, 0, GRID_W - NA_COLS)
    col_ok = (c[None, :] >= col_start[:, None]) & (c[None, :] < col_start[:, None] + NA_COLS)
    t = jnp.where(jnp.asarray(col_ok), t, -jnp.inf)
    t = t.transpose(0, 2, 1, 3, 4).reshape(depth, 2 * NA_ROWS - 1, HEADS * GRID_W, GRID_W)
    return jnp.concatenate([t[:, :-1], t[:, 1:]], axis=-1)


TB_LRU = 512
SUB = 8


def _gelu_tanh(x):
    return 0.5 * x * (1.0 + jnp.tanh(0.7978845608028654 * (x + 0.044715 * x * x * x)))


def _lru_kernel(x_ref, xp_ref, xn_ref, cw_ref, cb_ref, wa_ref, ba_ref, wx_ref, bx_ref, lam_ref,
                o_ref, ext_ref, ae_ref, ue_ref, cin_ref, carry_ref, *, reverse):
    TB = TB_LRU
    i = pl.program_id(1)
    nb = pl.num_programs(1)
    seq_blk = (nb - 1 - i) if reverse else i

    @pl.when(i == 0)
    def _():
        carry_ref[...] = jnp.zeros_like(carry_ref)

    ext_ref[0:SUB, :] = jnp.where(seq_blk == 0, 0.0, xp_ref[...])
    ext_ref[SUB:SUB + TB, :] = x_ref[...]
    ext_ref[SUB + TB:, :] = jnp.where(seq_blk == nb - 1, 0.0, xn_ref[...])
    xc = cb_ref[...] + jnp.zeros((TB, GW), F32)
    for j in range(LRU_CONV):
        xc = xc + ext_ref[pl.ds(SUB - 2 + j, TB), :] * cw_ref[j:j + 1, :]

    xcb = xc.astype(BF16)
    r = _sigmoid(_dot(xcb, wa_ref[...]) + ba_ref[...])
    gi = _sigmoid(_dot(xcb, wx_ref[...]) + bx_ref[...])
    log_a = (-LRU_C) * r * _softplus(-lam_ref[...])
    a = jnp.exp(log_a)
    u = jnp.sqrt(-jnp.tanh(log_a) * (a * a + 1.0)) * (gi * xc)

    a = a.reshape(TB // SUB, SUB, GW)
    u = u.reshape(TB // SUB, SUB, GW)
    rm = lax.broadcasted_iota(jnp.int32, (TB // SUB, SUB, GW), 1)
    for d in (1, 2, 4):
        shift, ok = (SUB - d, rm < SUB - d) if reverse else (d, rm >= d)
        a_sh, u_sh = pltpu.roll(a, shift, 1), pltpu.roll(u, shift, 1)
        u = jnp.where(ok, a * u_sh + u, u)
        a = jnp.where(ok, a * a_sh, a)
    nt = TB // SUB
    edge = 0 if reverse else SUB - 1
    ae_ref[...] = jnp.broadcast_to(a[:, edge:edge + 1, :], a.shape).reshape(TB, GW)
    ue_ref[...] = jnp.broadcast_to(u[:, edge:edge + 1, :], u.shape).reshape(TB, GW)

    def body(t, carry):
        off = pl.multiple_of(((nt - 1 - t) if reverse else t) * SUB, SUB)
        cin_ref[pl.ds(off, SUB), :] = carry
        return ue_ref[pl.ds(off, SUB), :] + ae_ref[pl.ds(off, SUB), :] * carry

    carry_ref[...] = lax.fori_loop(0, nt, body, carry_ref[...], unroll=8)

    o_ref[...] = u.reshape(TB, GW) + a.reshape(TB, GW) * cin_ref[...]


def _lru_dir(pa, w, l, batch, seq, reverse):
    nb = seq // TB_LRU
    n = batch * seq
    tiles = TB_LRU // SUB
    e = 1 if reverse else 0

    def rb(b, i):
        return b * nb + ((nb - 1 - i) if reverse else i)

    per_layer = lambda rows: pl.BlockSpec((None, rows, GW), lambda b, i: (l, 0, 0))
    per_dir = lambda rows: pl.BlockSpec((None, None, rows, GW), lambda b, i: (l, e, 0, 0))
    in_specs = [
        pl.BlockSpec((TB_LRU, GW), lambda b, i: (rb(b, i), CB_XC)),
        pl.BlockSpec((SUB, GW), lambda b, i: (jnp.maximum(rb(b, i) * tiles - 1, 0), CB_XC)),
        pl.BlockSpec((SUB, GW), lambda b, i: (jnp.minimum((rb(b, i) + 1) * tiles, n // SUB - 1), CB_XC)),
        per_layer(LRU_CONV), per_layer(1), per_dir(GW), per_dir(1), per_dir(GW), per_dir(1), per_dir(1),
    ]
    args = [pa, pa, pa, w["lru_conv_w"], w["lru_conv_b"], w["lru_wa"], w["lru_ba"], w["lru_wx"], w["lru_bx"],
            w["lru_lam"]]
    return pl.pallas_call(
        functools.partial(_lru_kernel, reverse=reverse),
        grid=(batch, nb),
        in_specs=in_specs,
        out_specs=pl.BlockSpec((TB_LRU, GW), lambda b, i: (rb(b, i), 0)),
        out_shape=jax.ShapeDtypeStruct((n, GW), F32),
        scratch_shapes=[pltpu.VMEM((TB_LRU + 2 * SUB, GW), F32)] + [pltpu.VMEM((TB_LRU, GW), F32)] * 3
                       + [pltpu.VMEM((SUB, GW), F32)],
        compiler_params=_params(("arbitrary", "arbitrary")),
        name="lru_bwd" if reverse else "lru_fwd",
    )(*args)


TJ_DIL = 512
SB_DIL = 128
WIN_DIL = SB_DIL + 2 * DIL_RADIUS


def _band_masks():
    q = np.arange(HEADS * SB_DIL)[:, None] % SB_DIL
    c = np.arange(WIN_DIL)[None, :]
    offs = (0, -DIL_RADIUS, -2 * DIL_RADIUS)
    return jnp.asarray(np.stack([np.where(np.abs(c + off - q) <= DIL_RADIUS, 0.0, -np.inf) for off in offs]), F32)


def _dil_kernel(*refs, sub_len, merge):
    if merge:
        (q_ref, k_ref, v_ref, mask_ref, o4_ref, l4_ref, o16_ref, l16_ref, o_ref,
         n4o_ref, n4l_ref, n16o_ref, n16l_ref) = refs
        for d, src_o, src_l, dst_o, dst_l in ((4, o4_ref, l4_ref, n4o_ref, n4l_ref),
                                              (16, o16_ref, l16_ref, n16o_ref, n16l_ref)):
            for r in range(d):
                for t in range(GW // LANES):
                    lanes = slice(t * LANES, (t + 1) * LANES)
                    dst_o[t, pl.ds(r, TJ_DIL // d, stride=d), :] = src_o[r, :, lanes]
                    dst_l[t, pl.ds(r, TJ_DIL // d, stride=d), :] = src_l[r, :, lanes]
    else:
        q_ref, k_ref, v_ref, mask_ref, o_ref, l_ref = refs
    j_blk = pl.program_id(2) * TJ_DIL
    scores, starts = [], []
    for sb in range(TJ_DIL // SB_DIL):
        rows = slice(sb * SB_DIL, (sb + 1) * SB_DIL)
        j0 = j_blk + sb * SB_DIL
        ws = pl.multiple_of(jnp.clip(j0 - DIL_RADIUS, 0, sub_len - WIN_DIL), DIL_RADIUS)
        which = 1 - (j0 == 0).astype(jnp.int32) + (j0 == sub_len - SB_DIL).astype(jnp.int32)
        s = _dot_nt(_stack_heads(q_ref[rows, :]), k_ref[pl.ds(ws, WIN_DIL), :])
        scores.append(s + mask_ref[which])
        starts.append(ws)
    results = _softmax_pv(scores, [v_ref[pl.ds(ws, WIN_DIL), :] for ws in starts])
    for sb, (o, lse) in enumerate(results):
        rows = slice(sb * SB_DIL, (sb + 1) * SB_DIL)
        if merge:
            tok = lambda ref: jnp.concatenate([ref[t, rows, :] for t in range(GW // LANES)], axis=1)
            l4, l16 = tok(n4l_ref), tok(n16l_ref)
            m = jnp.maximum(jnp.maximum(lse, l4), l16)
            e1, e4, e16 = jnp.exp(lse - m), jnp.exp(l4 - m), jnp.exp(l16 - m)
            o_ref[rows, :] = (e1 * o + e4 * tok(n4o_ref) + e16 * tok(n16o_ref)) / (e1 + e4 + e16)
        else:
            o_ref[rows, :] = o
            l_ref[rows, :] = lse


def _dil_branch(qkv, masks, batch, seq, dil, others=None):
    sub_len = seq // dil
    nj = sub_len // TJ_DIL
    col = lambda cb, rows, jmap: pl.BlockSpec((None, None, rows, GW), lambda b, r, j: (b, r, jmap(j), cb))
    in_specs = [col(0, TJ_DIL, lambda j: j), col(1, sub_len, lambda j: 0), col(2, sub_len, lambda j: 0),
                pl.BlockSpec(masks.shape, lambda b, r, j: (0, 0, 0))]
    out_blk = pl.BlockSpec((None, None, TJ_DIL, GW), lambda b, r, j: (b, r, j, 0))
    out_sd = jax.ShapeDtypeStruct((batch, dil, sub_len, GW), F32)
    args = [qkv, qkv, qkv, masks]
    scratch = []
    if others is None:
        out_specs, out_shape = [out_blk, out_blk], [out_sd, out_sd]
    else:
        assert dil == 1
        for d in (4, 16):
            in_specs += [pl.BlockSpec((None, d, TJ_DIL // d, GW), lambda b, r, j: (b, 0, j, 0))] * 2
        args += list(others)
        out_specs = pl.BlockSpec((TJ_DIL, GW), lambda b, r, j: (b * nj + j, 0))
        out_shape = jax.ShapeDtypeStruct((batch * seq, GW), F32)
        scratch = [pltpu.VMEM((GW // LANES, TJ_DIL, LANES), F32)] * 4
    return pl.pallas_call(
        functools.partial(_dil_kernel, sub_len=sub_len, merge=others is not None),
        grid=(batch, dil, nj),
        in_specs=in_specs,
        out_specs=out_specs,
        out_shape=out_shape,
        scratch_shapes=scratch,
        compiler_params=_params(("parallel", "parallel", "arbitrary")),
        name=f"dil_attn_d{dil}",
    )(*args)


def _reorder_w_in(w_in):
    qa = 0
    za = 4 * GW
    qb = za + 2 * GLA_RANK
    xc = qb + QKV
    qd = xc + 2 * GW
    end = qd + QKV
    pad = jnp.zeros(w_in.shape[:-1] + (LANES - 2 * GLA_RANK,), w_in.dtype)
    return jnp.concatenate([w_in[..., qa:za], w_in[..., xc:qd], w_in[..., za:qb], pad,
                            w_in[..., qb:xc], w_in[..., qd:end]], axis=-1).astype(BF16)


def _block_diag(w):
    eye = jnp.eye(HEADS, dtype=w.dtype)[:, None, :, None]
    out = w[..., :, :, None, :] * eye
    return out.reshape(w.shape[:-3] + (GW, GW))


def _gate_weights(w_gate):
    per_dir = [jnp.pad(w_gate[:, e], ((0, 0), (e * GLA_RANK, LANES - (e + 1) * GLA_RANK), (0, 0))) for e in (0, 1)]
    return jnp.stack(per_dir, axis=1).astype(BF16)


def _rope_tables(seq):
    pos = jnp.arange(seq, dtype=F32)
    inv_freq = ROPE_THETA ** (-jnp.arange(0, HEAD_DIM, 2, dtype=F32) / HEAD_DIM)
    ang = pos[:, None] * inv_freq[None, :]
    cos, sin = jnp.cos(ang), jnp.sin(ang)
    return jnp.tile(jnp.concatenate([cos, cos], -1), (1, HEADS)), jnp.tile(jnp.concatenate([-sin, sin], -1), (1, HEADS))


def kernel(x, mix_norm_pre, mix_norm_post, w_in, gla_w_gate, gla_b_gate, gla_norm, na_rpb, lru_conv_w, lru_conv_b, lru_w_a, lru_b_a, lru_w_x, lru_b_x, lru_lambda, w_out, ffn_norm_pre, ffn_norm_post, ffn_w_in, ffn_w_out):
    batch, seq, d_model = x.shape
    assert d_model == D_MODEL and seq % (16 * TJ_DIL) == 0 and seq % TB_GLA == 0
    depth = w_in.shape[0]
    n = batch * seq
    xf = x.reshape(n, D_MODEL)
    cos_t, sin_t = _rope_tables(seq)
    masks = _band_masks()
    rows = lambda t: t[..., None, :]
    w = dict(
        mix_pre=rows(mix_norm_pre), mix_post=rows(mix_norm_post), w_in=_reorder_w_in(w_in),
        gla_wg=_gate_weights(gla_w_gate), gla_bg=rows(gla_b_gate), gla_norm=rows(gla_norm),
        nbr_bias=_nbr_bias(na_rpb),
        lru_conv_w=lru_conv_w, lru_conv_b=rows(lru_conv_b),
        lru_wa=_block_diag(lru_w_a).astype(BF16), lru_ba=rows(lru_b_a),
        lru_wx=_block_diag(lru_w_x).astype(BF16), lru_bx=rows(lru_b_x), lru_lam=rows(lru_lambda),
        w_out=w_out.astype(BF16), ffn_pre=rows(ffn_norm_pre), ffn_post=rows(ffn_norm_post),
        ffn_in=ffn_w_in.astype(BF16), ffn_out=ffn_w_out.astype(BF16),
    )

    for l in range(depth):
        pa, pb, d1, d4, d16 = _in_proj(xf, w, l, cos_t, sin_t, batch, seq)

        gla_dirs = _gla(pa, w, l, batch, seq)
        yb = _nbr(pb, w, l, batch, seq)
        lru_dirs = (_lru_dir(pa, w, l, batch, seq, False), _lru_dir(pa, w, l, batch, seq, True))
        o4, l4 = _dil_branch(d4, masks, batch, seq, 4)
        o16, l16 = _dil_branch(d16, masks, batch, seq, 16)
        yd = _dil_branch(d1, masks, batch, seq, 1, others=(o4, l4, o16, l16))

        xf = _tail(pa, gla_dirs, yb, lru_dirs, yd, xf, w, l)
    return xf.reshape(batch, seq, D_MODEL)
```

```python
import functools

import numpy as np
import jax
import jax.numpy as jnp
from jax import lax
from jax.experimental import pallas as pl
from jax.experimental.pallas import tpu as pltpu

F32 = jnp.float32
BF16 = jnp.bfloat16

D_MODEL = 1024
HEAD_DIM = 64
HEADS = 4
GW = HEADS * HEAD_DIM
GLA_RANK = 16
GLA_TAU = 16.0
GLA_CHUNK = 64
GRID_W = 64
NA_ROWS = 8
NA_COLS = 16
LRU_C = 8.0
LRU_CONV = 4
DILATIONS = (1, 4, 16)
DIL_RADIUS = 64
ROPE_THETA = 10000.0
D_FF = 2816
EPS = 1e-6
QK_SCALE = HEAD_DIM ** -0.5
LANES = 128

CB_QA, CB_KA, CB_VA, CB_GA, CB_XC, CB_GC = 0, 1, 2, 3, 4, 5
Z_COL0 = 6 * GW
PA_COLS = Z_COL0 + LANES
QKV = 3 * GW
P_COLS = PA_COLS + 2 * QKV

VMEM_LIMIT = 56 * 1024 * 1024


def _params(sem, vmem=VMEM_LIMIT):
    return pltpu.CompilerParams(dimension_semantics=sem, vmem_limit_bytes=vmem)


def _stack_heads(t):
    rows = t.shape[0]
    lane = lax.broadcasted_iota(jnp.int32, (rows, LANES), 1)
    zero = jnp.zeros((rows, LANES), t.dtype)
    blocks = []
    for h in range(HEADS):
        half = t[:, (h // 2) * LANES:(h // 2 + 1) * LANES]
        half = jnp.where((lane < HEAD_DIM) if h % 2 == 0 else (lane >= HEAD_DIM), half, zero)
        blocks.append(jnp.concatenate([half, zero] if h < 2 else [zero, half], axis=1))
    return jnp.concatenate(blocks, axis=0)


def _unstack_heads(t):
    rows = t.shape[0] // HEADS
    first = lax.broadcasted_iota(jnp.int32, (rows, LANES), 1) < HEAD_DIM
    blk = lambda h, tile: t[h * rows:(h + 1) * rows, tile * LANES:(tile + 1) * LANES]
    return jnp.concatenate([jnp.where(first, blk(0, 0), blk(1, 0)), jnp.where(first, blk(2, 1), blk(3, 1))], axis=1)


def _dot(a, b):
    return jnp.dot(a, b, preferred_element_type=F32)


def _dot_nt(a, b):
    return lax.dot_general(a, b, (((1,), (1,)), ((), ())), preferred_element_type=F32)


def _dot_tn(a, b):
    return lax.dot_general(a, b, (((0,), (0,)), ((), ())), preferred_element_type=F32)


def _split(t):
    hi = t.astype(BF16)
    return hi, (t - hi.astype(F32)).astype(BF16)


def _split_dot(m, t):
    hi, lo = _split(t)
    return _dot(m, hi) + _dot(m, lo)


def _split_dot_rhs(t, m):
    hi, lo = _split(t)
    return _dot(hi, m) + _dot(lo, m)


def _rms(x, g):
    ms = jnp.mean(x * x, axis=-1, keepdims=True)
    return x * lax.rsqrt(ms + EPS) * g


def _sigmoid(x):
    return 1.0 / (1.0 + jnp.exp(-x))


def _softplus(x):
    return jnp.maximum(x, 0.0) + jnp.log1p(jnp.exp(-jnp.abs(x)))


def _softmax_pv(scores, values):
    maxes = [jnp.max(s, axis=-1, keepdims=True) for s in scores]
    exps = [jnp.exp(s - m) for s, m in zip(scores, maxes)]
    sums = [jnp.sum(e, axis=-1, keepdims=True) for e in exps]
    pvs = [_dot(e.astype(BF16), v) for e, v in zip(exps, values)]
    out = []
    for pv, m, d in zip(pvs, maxes, sums):
        den = _unstack_heads(jnp.broadcast_to(d, pv.shape))
        o = _unstack_heads(pv) / den
        lse = _unstack_heads(jnp.broadcast_to(m, pv.shape)) + jnp.log(den)
        out.append((o, lse))
    return out


TM_IN = 512


def _rope(t, c, s):
    lane = lax.broadcasted_iota(jnp.int32, t.shape, 1)
    first_half = (lane & (HEAD_DIM - 1)) < HEAD_DIM // 2
    swapped = jnp.where(first_half, pltpu.roll(t, GW - HEAD_DIM // 2, 1), pltpu.roll(t, HEAD_DIM // 2, 1))
    return t * c + swapped * s


def _in_proj_kernel(x_ref, g_ref, w_ref, c_ref, s_ref, pa_ref, pb_ref, d1_ref, d4_ref, d16_ref, rope_ref):
    h = _rms(x_ref[...], g_ref[...]).astype(BF16)
    pd = _dot(h, w_ref[:, PA_COLS + QKV:P_COLS])
    pb = _dot(h, w_ref[:, PA_COLS:PA_COLS + QKV])
    pb_ref[:, 0:GW] = (pb[:, 0:GW] * QK_SCALE).astype(BF16)
    pb_ref[:, GW:QKV] = pb[:, GW:QKV].astype(BF16)
    c = c_ref[...]
    s = s_ref[...]
    pa_ref[...] = _dot(h, w_ref[:, 0:PA_COLS])
    qkv = jnp.concatenate([_rope(pd[:, 0:GW], c, s) * QK_SCALE, _rope(pd[:, GW:2 * GW], c, s),
                           pd[:, 2 * GW:QKV]], axis=1)
    d1_ref[0] = qkv.astype(BF16)
    for t in range(QKV // LANES):
        lanes = slice(t * LANES, (t + 1) * LANES)
        rope_ref[t] = qkv[:, lanes]
        for d, ref in ((4, d4_ref), (16, d16_ref)):
            for r in range(d):
                ref[r, :, lanes] = rope_ref[t, pl.ds(r, TM_IN // d, stride=d), :].astype(BF16)


def _in_proj(x, w, l, cos_t, sin_t, batch, seq):
    n = x.shape[0]
    nl = seq // TM_IN
    row = lambda cols: pl.BlockSpec((TM_IN, cols), lambda i: (i, 0))
    tab = pl.BlockSpec((TM_IN, GW), lambda i: (i % nl, 0))
    res = lambda d: pl.BlockSpec((None, d, TM_IN // d, QKV), lambda i: (i // nl, 0, i % nl, 0))
    return pl.pallas_call(
        _in_proj_kernel,
        grid=(n // TM_IN,),
        in_specs=[row(D_MODEL),
                  pl.BlockSpec((None, 1, D_MODEL), lambda i: (l, 0, 0)),
                  pl.BlockSpec((None, D_MODEL, P_COLS), lambda i: (l, 0, 0)),
                  tab, tab],
        out_specs=[row(PA_COLS), row(QKV), res(1), res(4), res(16)],
        out_shape=[jax.ShapeDtypeStruct((n, PA_COLS), F32),
                   jax.ShapeDtypeStruct((n, QKV), BF16),
                   jax.ShapeDtypeStruct((batch, 1, seq, QKV), BF16),
                   jax.ShapeDtypeStruct((batch, 4, seq // 4, QKV), BF16),
                   jax.ShapeDtypeStruct((batch, 16, seq // 16, QKV), BF16)],
        scratch_shapes=[pltpu.VMEM((QKV // LANES, TM_IN, LANES), F32)],
        compiler_params=_params(("parallel",)),
        name="in_proj",
    )(x, w["mix_pre"], w["w_in"], cos_t, sin_t)


TM_FFN = 512
MXU_TILE = 256
FF_CHUNKS = ((0, 6 * MXU_TILE), (6 * MXU_TILE, D_FF))


def _tail_kernel(af_ref, ab_ref, ag_ref, an_ref, yb_ref, cf_ref, cb_ref, cg_ref, yd_ref,
                 wm_ref, x_ref, gm_ref, g1_ref, wi_ref, wo_ref, g2_ref, o_ref):
    ya = _gla_finalize(af_ref[...] + ab_ref[...], ag_ref[...], an_ref[...])
    yc = (cf_ref[...] + cb_ref[...]) * _gelu_tanh(cg_ref[...])
    y = _dot(ya.astype(BF16), wm_ref[0 * GW:1 * GW, :])
    y += _dot(yb_ref[...].astype(BF16), wm_ref[1 * GW:2 * GW, :])
    y += _dot(yc.astype(BF16), wm_ref[2 * GW:3 * GW, :])
    y += _dot(yd_ref[...].astype(BF16), wm_ref[3 * GW:4 * GW, :])
    x1 = x_ref[...] + _rms(y, gm_ref[...])
    h = _rms(x1, g1_ref[...]).astype(BF16)
    f = None
    for lo, hi in FF_CHUNKS:
        gate = _dot(h, wi_ref[:, lo:hi])
        up = _dot(h, wi_ref[:, D_FF + lo:D_FF + hi])
        part = _dot((gate * _sigmoid(gate) * up).astype(BF16), wo_ref[lo:hi, :])
        f = part if f is None else f + part
    o_ref[...] = x1 + _rms(f, g2_ref[...])


def _tail(pa, gla_dirs, yb, lru_dirs, yd, x, w, l):
    n = x.shape[0]
    row = pl.BlockSpec((TM_FFN, GW), lambda i: (i, 0))
    stream = lambda cb: pl.BlockSpec((TM_FFN, GW), lambda i: (i, cb))
    gain = pl.BlockSpec((None, 1, D_MODEL), lambda i: (l, 0, 0))
    resident = lambda r, c: pl.BlockSpec((None, r, c), lambda i: (l, 0, 0), pipeline_mode=pl.Buffered(1))
    return pl.pallas_call(
        _tail_kernel,
        grid=(n // TM_FFN,),
        in_specs=[row, row, stream(CB_GA), pl.BlockSpec((None, 1, GW), lambda i: (l, 0, 0)), row,
                  row, row, stream(CB_GC), row] + [
            resident(D_MODEL, D_MODEL),
            pl.BlockSpec((TM_FFN, D_MODEL), lambda i: (i, 0)),
            gain, gain,
            resident(D_MODEL, 2 * D_FF),
            resident(D_FF, D_MODEL),
            gain,
        ],
        out_specs=pl.BlockSpec((TM_FFN, D_MODEL), lambda i: (i, 0)),
        out_shape=jax.ShapeDtypeStruct((n, D_MODEL), F32),
        compiler_params=_params(("parallel",)),
        name="out_proj_ffn",
    )(gla_dirs[0], gla_dirs[1], pa, w["gla_norm"], yb, lru_dirs[0], lru_dirs[1], pa, yd,
      w["w_out"], x, w["mix_post"], w["ffn_pre"], w["ffn_in"], w["ffn_out"], w["ffn_post"])


TB_GLA = 512


def _same_head_tile():
    rb = lax.broadcasted_iota(jnp.int32, (LANES, LANES), 0) >> 6
    cb = lax.broadcasted_iota(jnp.int32, (LANES, LANES), 1) >> 6
    return rb == cb


def _gla_direction(q_ref, k_ref, v_ref, z_ref, wg_ref, bg_ref, o_ref, st_ref, reverse):
    C = GLA_CHUNK
    n_chunks = TB_GLA // C
    logit = _dot(z_ref[...].astype(BF16), wg_ref[...]) + bg_ref[...]
    log_a = -_softplus(-logit) * (1.0 / GLA_TAU)

    r_i = lax.broadcasted_iota(jnp.int32, (C, C), 0)
    c_i = lax.broadcasted_iota(jnp.int32, (C, C), 1)
    cum = jnp.where((c_i >= r_i) if reverse else (c_i <= r_i), 1.0, 0.0).astype(BF16)
    t_i = lax.broadcasted_iota(jnp.int32, (C, LANES), 0)
    s_i = lax.broadcasted_iota(jnp.int32, (C, LANES), 1) & (C - 1)
    causal = (s_i >= t_i) if reverse else (s_i <= t_i)
    same_head = _same_head_tile()
    last, mid = (0, C // 2) if reverse else (C - 1, C // 2 - 1)

    def stack2(t):
        lane = lax.broadcasted_iota(jnp.int32, t.shape, 1)
        zero = jnp.zeros_like(t)
        return jnp.concatenate([jnp.where(lane < HEAD_DIM, t, zero), jnp.where(lane >= HEAD_DIM, t, zero)], axis=0)

    b_c = [_split_dot(cum, log_a[c * C:(c + 1) * C]) for c in range(n_chunks)]
    b = jnp.concatenate(b_c, axis=0)
    b_mid = jnp.concatenate([jnp.broadcast_to(t[mid:mid + 1], (C, GW)) for t in b_c], axis=0)
    q = q_ref[...] * QK_SCALE
    k = k_ref[...]
    v = v_ref[...].astype(BF16)
    q_mid = q * jnp.exp(b - b_mid)
    k_mid = k * jnp.exp(b_mid - b)
    q_in = q_mid.astype(BF16)
    k_in = k_mid.astype(BF16)
    q_st = jnp.concatenate([q_mid[c * C:(c + 1) * C] * jnp.exp(t[mid:mid + 1]) for c, t in enumerate(b_c)],
                           axis=0).astype(BF16)
    k_st = jnp.concatenate([k_mid[c * C:(c + 1) * C] * jnp.exp(t[last:last + 1] - t[mid:mid + 1])
                            for c, t in enumerate(b_c)], axis=0).astype(BF16)

    yield

    chunk = [slice(c * C, (c + 1) * C) for c in range(n_chunks)]
    order = range(n_chunks - 1, -1, -1) if reverse else range(n_chunks)
    for tile in range(GW // LANES):
        lanes = slice(tile * LANES, (tile + 1) * LANES)
        att = [_dot_nt(q_in[r, lanes], stack2(k_in[r, lanes])) for r in chunk]
        yield
        o_in = [_dot(jnp.where(causal, a, 0.0).astype(BF16), stack2(v[r, lanes])) for a, r in zip(att, chunk)]
        upd = [_dot_tn(v[r, lanes], k_st[r, lanes]) for r in chunk]
        yield
        st = st_ref[tile]
        for c in order:
            o_ref[chunk[c], lanes] = o_in[c] + _dot_nt(q_st[chunk[c], lanes], st.astype(BF16))
            st = st * jnp.exp(b_c[c][last:last + 1, lanes]) + jnp.where(same_head, upd[c], 0.0)
            yield
        st_ref[tile] = st


def _gla_kernel(qf_ref, kf_ref, vf_ref, zf_ref, qb_ref, kb_ref, vb_ref, zb_ref, wg_ref, bg_ref,
                of_ref, ob_ref, stf_ref, stb_ref):
    @pl.when(pl.program_id(1) == 0)
    def _():
        stf_ref[...] = jnp.zeros_like(stf_ref)
        stb_ref[...] = jnp.zeros_like(stb_ref)

    stages = [_gla_direction(qf_ref, kf_ref, vf_ref, zf_ref, wg_ref.at[0], bg_ref.at[0], of_ref, stf_ref, False),
              _gla_direction(qb_ref, kb_ref, vb_ref, zb_ref, wg_ref.at[1], bg_ref.at[1], ob_ref, stb_ref, True)]
    while stages:
        stages = [s for s in stages if next(s, True) is None]


def _gla_finalize(o, g, norm_g):
    blk = jnp.where(_same_head_tile(), 1.0 / HEAD_DIM, 0.0).astype(BF16)
    sq = o * o
    ms = jnp.concatenate([_split_dot_rhs(sq[:, t * LANES:(t + 1) * LANES], blk) for t in range(GW // LANES)],
                         axis=1)
    return o * lax.rsqrt(ms + EPS) * norm_g * (g * _sigmoid(g))


def _gla(pa, w, l, batch, seq):
    nb = seq // TB_GLA
    n = batch * seq
    fwd = lambda b, i: b * nb + i
    bwd = lambda b, i: b * nb + nb - 1 - i
    col = lambda rb, cb: pl.BlockSpec((TB_GLA, GW), lambda b, i: (rb(b, i), cb))
    gate = lambda rb: pl.BlockSpec((TB_GLA, LANES), lambda b, i: (rb(b, i), Z_COL0 // LANES))
    out_sd = jax.ShapeDtypeStruct((n, GW), F32)
    state = pltpu.VMEM((GW // LANES, LANES, LANES), F32)
    return pl.pallas_call(
        _gla_kernel,
        grid=(batch, nb),
        in_specs=[col(fwd, CB_QA), col(fwd, CB_KA), col(fwd, CB_VA), gate(fwd),
                  col(bwd, CB_QA), col(bwd, CB_KA), col(bwd, CB_VA), gate(bwd),
                  pl.BlockSpec((None, 2, LANES, GW), lambda b, i: (l, 0, 0, 0)),
                  pl.BlockSpec((None, 2, 1, GW), lambda b, i: (l, 0, 0, 0))],
        out_specs=[col(fwd, 0), col(bwd, 0)],
        out_shape=[out_sd, out_sd],
        scratch_shapes=[state, state],
        compiler_params=_params(("arbitrary", "arbitrary")),
        name="gla",
    )(pa, pa, pa, pa, pa, pa, pa, pa, w["gla_wg"], w["gla_bg"])


NA_BLOCK_ROWS = 16
NA_GROUP = 8
NA_PAIRS = 2 * NA_ROWS - 2


def _nbr_kernel(q_ref, k_ref, v_ref, bias_ref, o_ref, *, grid_rows):
    W = GRID_W
    win = NA_ROWS * W
    row0 = pl.program_id(1) * NA_BLOCK_ROWS

    def body(g, carry):
        scores, starts = [], []
        for j in range(NA_GROUP):
            r = row0 + g * NA_GROUP + j
            start = jnp.clip(r - NA_ROWS // 2, 0, grid_rows - NA_ROWS)
            pat = start - r + (NA_ROWS - 1)
            k0 = pl.multiple_of(start * W, W)
            q0 = pl.multiple_of((g * NA_GROUP + j) * W, W)
            s = _dot_nt(_stack_heads(q_ref[pl.ds(q0, W), :]), k_ref[pl.ds(k0, win), :])
            scores.append(s + jnp.concatenate([bias_ref[pat + 2 * t] for t in range(NA_ROWS // 2)], axis=1))
            starts.append(k0)
        results = _softmax_pv(scores, [v_ref[pl.ds(k0, win), :] for k0 in starts])
        for j, (o, _) in enumerate(results):
            o_ref[pl.ds(pl.multiple_of((g * NA_GROUP + j) * W, W), W), :] = o
        return carry

    lax.fori_loop(0, NA_BLOCK_ROWS // NA_GROUP, body, 0)


def _nbr(pb, w, l, batch, seq):
    grid_rows = seq // GRID_W
    tq = NA_BLOCK_ROWS * GRID_W
    nb = seq // tq
    return pl.pallas_call(
        functools.partial(_nbr_kernel, grid_rows=grid_rows),
        grid=(batch, nb),
        in_specs=[
            pl.BlockSpec((tq, GW), lambda b, i: (b * nb + i, 0)),
            pl.BlockSpec((seq, GW), lambda b, i: (b, 1)),
            pl.BlockSpec((seq, GW), lambda b, i: (b, 2)),
            pl.BlockSpec((None, NA_PAIRS, HEADS * GRID_W, 2 * GRID_W), lambda b, i: (l, 0, 0, 0)),
        ],
        out_specs=pl.BlockSpec((tq, GW), lambda b, i: (b * nb + i, 0)),
        out_shape=jax.ShapeDtypeStruct((batch * seq, GW), F32),
        compiler_params=_params(("parallel", "arbitrary")),
        name="nbr_attn",
    )(pb, pb, pb, w["nbr_bias"])


def _nbr_bias(rpb):
    depth = rpb.shape[0]
    c = np.arange(GRID_W)
    dc = np.clip(c[None, :] - c[:, None], -(NA_COLS - 1), NA_COLS - 1) + NA_COLS - 1
    onehot = jnp.asarray(dc.reshape(-1)[None, :] == np.arange(2 * NA_COLS - 1)[:, None], F32)
    t = jnp.dot(rpb.reshape(-1, 2 * NA_COLS - 1), onehot, precision=lax.Precision.HIGHEST)
    t = t.reshape(depth, HEADS, 2 * NA_ROWS - 1, GRID_W, GRID_W)
    col_start = np.clip(c - NA_COLS // 2, 0, GRID_W - NA_COLS)
    col_ok = (c[None, :] >= col_start[:, None]) & (c[None, :] < col_start[:, None] + NA_COLS)
    t = jnp.where(jnp.asarray(col_ok), t, -jnp.inf)
    t = t.transpose(0, 2, 1, 3, 4).reshape(depth, 2 * NA_ROWS - 1, HEADS * GRID_W, GRID_W)
    return jnp.concatenate([t[:, :-1], t[:, 1:]], axis=-1)


TB_LRU = 512
SUB = 8


def _gelu_tanh(x):
    return 0.5 * x * (1.0 + jnp.tanh(0.7978845608028654 * (x + 0.044715 * x * x * x)))


def _lru_kernel(x_ref, xp_ref, xn_ref, cw_ref, cb_ref, wa_ref, ba_ref, wx_ref, bx_ref, lam_ref,
                o_ref, ext_ref, ae_ref, ue_ref, cin_ref, carry_ref, *, reverse):
    TB = TB_LRU
    i = pl.program_id(1)
    nb = pl.num_programs(1)
    seq_blk = (nb - 1 - i) if reverse else i

    @pl.when(i == 0)
    def _():
        carry_ref[...] = jnp.zeros_like(carry_ref)

    ext_ref[0:SUB, :] = jnp.where(seq_blk == 0, 0.0, xp_ref[...])
    ext_ref[SUB:SUB + TB, :] = x_ref[...]
    ext_ref[SUB + TB:, :] = jnp.where(seq_blk == nb - 1, 0.0, xn_ref[...])
    xc = cb_ref[...] + jnp.zeros((TB, GW), F32)
    for j in range(LRU_CONV):
        xc = xc + ext_ref[pl.ds(SUB - 2 + j, TB), :] * cw_ref[j:j + 1, :]

    xcb = xc.astype(BF16)
    r = _sigmoid(_dot(xcb, wa_ref[...]) + ba_ref[...])
    gi = _sigmoid(_dot(xcb, wx_ref[...]) + bx_ref[...])
    log_a = (-LRU_C) * r * _softplus(-lam_ref[...])
    a = jnp.exp(log_a)
    u = jnp.sqrt(-jnp.tanh(log_a) * (a * a + 1.0)) * (gi * xc)

    a = a.reshape(TB // SUB, SUB, GW)
    u = u.reshape(TB // SUB, SUB, GW)
    rm = lax.broadcasted_iota(jnp.int32, (TB // SUB, SUB, GW), 1)
    for d in (1, 2, 4):
        shift, ok = (SUB - d, rm < SUB - d) if reverse else (d, rm >= d)
        a_sh, u_sh = pltpu.roll(a, shift, 1), pltpu.roll(u, shift, 1)
        u = jnp.where(ok, a * u_sh + u, u)
        a = jnp.where(ok, a * a_sh, a)
    nt = TB // SUB
    edge = 0 if reverse else SUB - 1
    ae_ref[...] = jnp.broadcast_to(a[:, edge:edge + 1, :], a.shape).reshape(TB, GW)
    ue_ref[...] = jnp.broadcast_to(u[:, edge:edge + 1, :], u.shape).reshape(TB, GW)

    def body(t, carry):
        off = pl.multiple_of(((nt - 1 - t) if reverse else t) * SUB, SUB)
        cin_ref[pl.ds(off, SUB), :] = carry
        return ue_ref[pl.ds(off, SUB), :] + ae_ref[pl.ds(off, SUB), :] * carry

    carry_ref[...] = lax.fori_loop(0, nt, body, carry_ref[...], unroll=8)

    o_ref[...] = u.reshape(TB, GW) + a.reshape(TB, GW) * cin_ref[...]


def _lru_dir(pa, w, l, batch, seq, reverse):
    nb = seq // TB_LRU
    n = batch * seq
    tiles = TB_LRU // SUB
    e = 1 if reverse else 0

    def rb(b, i):
        return b * nb + ((nb - 1 - i) if reverse else i)

    per_layer = lambda rows: pl.BlockSpec((None, rows, GW), lambda b, i: (l, 0, 0))
    per_dir = lambda rows: pl.BlockSpec((None, None, rows, GW), lambda b, i: (l, e, 0, 0))
    in_specs = [
        pl.BlockSpec((TB_LRU, GW), lambda b, i: (rb(b, i), CB_XC)),
        pl.BlockSpec((SUB, GW), lambda b, i: (jnp.maximum(rb(b, i) * tiles - 1, 0), CB_XC)),
        pl.BlockSpec((SUB, GW), lambda b, i: (jnp.minimum((rb(b, i) + 1) * tiles, n // SUB - 1), CB_XC)),
        per_layer(LRU_CONV), per_layer(1), per_dir(GW), per_dir(1), per_dir(GW), per_dir(1), per_dir(1),
    ]
    args = [pa, pa, pa, w["lru_conv_w"], w["lru_conv_b"], w["lru_wa"], w["lru_ba"], w["lru_wx"], w["lru_bx"],
            w["lru_lam"]]
    return pl.pallas_call(
        functools.partial(_lru_kernel, reverse=reverse),
        grid=(batch, nb),
        in_specs=in_specs,
        out_specs=pl.BlockSpec((TB_LRU, GW), lambda b, i: (rb(b, i), 0)),
        out_shape=jax.ShapeDtypeStruct((n, GW), F32),
        scratch_shapes=[pltpu.VMEM((TB_LRU + 2 * SUB, GW), F32)] + [pltpu.VMEM((TB_LRU, GW), F32)] * 3
                       + [pltpu.VMEM((SUB, GW), F32)],
        compiler_params=_params(("arbitrary", "arbitrary")),
        name="lru_bwd" if reverse else "lru_fwd",
    )(*args)


SB_DIL = 128
GROUP_DIL = 4
WIN_DIL = SB_DIL + 2 * DIL_RADIUS
DIL_STEP = {1: (1024, 1), 4: (1024, 1), 16: (512, 2)}


def _band_masks():
    q = np.arange(HEADS * SB_DIL)[:, None] % SB_DIL
    c = np.arange(WIN_DIL)[None, :]
    offs = (0, -DIL_RADIUS, -2 * DIL_RADIUS)
    return jnp.asarray(np.stack([np.where(np.abs(c + off - q) <= DIL_RADIUS, 0.0, -np.inf) for off in offs]), F32)


def _dil_kernel(*refs, sub_len, tj, rps, merge):
    if merge:
        (q_ref, k_ref, v_ref, mask_ref, o4_ref, l4_ref, o16_ref, l16_ref, o_ref,
         n4o_ref, n4l_ref, n16o_ref, n16l_ref) = refs
        for d, src_o, src_l, dst_o, dst_l in ((4, o4_ref, l4_ref, n4o_ref, n4l_ref),
                                              (16, o16_ref, l16_ref, n16o_ref, n16l_ref)):
            for r in range(d):
                for t in range(GW // LANES):
                    lanes = slice(t * LANES, (t + 1) * LANES)
                    dst_o[t, pl.ds(r, tj // d, stride=d), :] = src_o[r, :, lanes]
                    dst_l[t, pl.ds(r, tj // d, stride=d), :] = src_l[r, :, lanes]
    else:
        q_ref, k_ref, v_ref, mask_ref, o_ref, l_ref = refs
    j_blk = pl.program_id(2) * tj
    blocks = [(rr, sb) for rr in range(rps) for sb in range(tj // SB_DIL)]
    for g in range(0, len(blocks), GROUP_DIL):
        group = blocks[g:g + GROUP_DIL]
        scores, values = [], []
        for rr, sb in group:
            rows = slice(sb * SB_DIL, (sb + 1) * SB_DIL)
            j0 = j_blk + sb * SB_DIL
            ws = pl.multiple_of(jnp.clip(j0 - DIL_RADIUS, 0, sub_len - WIN_DIL), DIL_RADIUS)
            which = 1 - (j0 == 0).astype(jnp.int32) + (j0 == sub_len - SB_DIL).astype(jnp.int32)
            s = _dot_nt(_stack_heads(q_ref[rr, rows, :]), k_ref[rr, pl.ds(ws, WIN_DIL), :])
            scores.append(s + mask_ref[which])
            values.append(v_ref[rr, pl.ds(ws, WIN_DIL), :])
        for (rr, sb), (o, lse) in zip(group, _softmax_pv(scores, values)):
            rows = slice(sb * SB_DIL, (sb + 1) * SB_DIL)
            if merge:
                tok = lambda ref: jnp.concatenate([ref[t, rows, :] for t in range(GW // LANES)], axis=1)
                l4, l16 = tok(n4l_ref), tok(n16l_ref)
                m = jnp.maximum(jnp.maximum(lse, l4), l16)
                e1, e4, e16 = jnp.exp(lse - m), jnp.exp(l4 - m), jnp.exp(l16 - m)
                o_ref[rows, :] = (e1 * o + e4 * tok(n4o_ref) + e16 * tok(n16o_ref)) / (e1 + e4 + e16)
            else:
                o_ref[rr, rows, :] = o
                l_ref[rr, rows, :] = lse


def _dil_branch(qkv, masks, batch, seq, dil, others=None):
    sub_len = seq // dil
    tj, rps = DIL_STEP[dil]
    nj = sub_len // tj
    col = lambda cb, rows, jmap: pl.BlockSpec((None, rps, rows, GW), lambda b, r, j: (b, r, jmap(j), cb))
    in_specs = [col(0, tj, lambda j: j), col(1, sub_len, lambda j: 0), col(2, sub_len, lambda j: 0),
                pl.BlockSpec(masks.shape, lambda b, r, j: (0, 0, 0))]
    out_blk = pl.BlockSpec((None, rps, tj, GW), lambda b, r, j: (b, r, j, 0))
    out_sd = jax.ShapeDtypeStruct((batch, dil, sub_len, GW), F32)
    args = [qkv, qkv, qkv, masks]
    scratch = []
    if others is None:
        out_specs, out_shape = [out_blk, out_blk], [out_sd, out_sd]
    else:
        assert dil == 1 and rps == 1
        for d in (4, 16):
            in_specs += [pl.BlockSpec((None, d, tj // d, GW), lambda b, r, j: (b, 0, j, 0))] * 2
        args += list(others)
        out_specs = pl.BlockSpec((tj, GW), lambda b, r, j: (b * nj + j, 0))
        out_shape = jax.ShapeDtypeStruct((batch * seq, GW), F32)
        scratch = [pltpu.VMEM((GW // LANES, tj, LANES), F32)] * 4
    return pl.pallas_call(
        functools.partial(_dil_kernel, sub_len=sub_len, tj=tj, rps=rps, merge=others is not None),
        grid=(batch, dil // rps, nj),
        in_specs=in_specs,
        out_specs=out_specs,
        out_shape=out_shape,
        scratch_shapes=scratch,
        compiler_params=_params(("parallel", "parallel", "arbitrary")),
        name=f"dil_attn_d{dil}",
    )(*args)


def _reorder_w_in(w_in):
    qa = 0
    za = 4 * GW
    qb = za + 2 * GLA_RANK
    xc = qb + QKV
    qd = xc + 2 * GW
    end = qd + QKV
    pad = jnp.zeros(w_in.shape[:-1] + (LANES - 2 * GLA_RANK,), w_in.dtype)
    return jnp.concatenate([w_in[..., qa:za], w_in[..., xc:qd], w_in[..., za:qb], pad,
                            w_in[..., qb:xc], w_in[..., qd:end]], axis=-1).astype(BF16)


def _block_diag(w):
    eye = jnp.eye(HEADS, dtype=w.dtype)[:, None, :, None]
    out = w[..., :, :, None, :] * eye
    return out.reshape(w.shape[:-3] + (GW, GW))


def _gate_weights(w_gate):
    per_dir = [jnp.pad(w_gate[:, e], ((0, 0), (e * GLA_RANK, LANES - (e + 1) * GLA_RANK), (0, 0))) for e in (0, 1)]
    return jnp.stack(per_dir, axis=1).astype(BF16)


def _rope_tables(seq):
    pos = jnp.arange(seq, dtype=F32)
    inv_freq = ROPE_THETA ** (-jnp.arange(0, HEAD_DIM, 2, dtype=F32) / HEAD_DIM)
    ang = pos[:, None] * inv_freq[None, :]
    cos, sin = jnp.cos(ang), jnp.sin(ang)
    return jnp.tile(jnp.concatenate([cos, cos], -1), (1, HEADS)), jnp.tile(jnp.concatenate([-sin, sin], -1), (1, HEADS))


def kernel(x, mix_norm_pre, mix_norm_post, w_in, gla_w_gate, gla_b_gate, gla_norm, na_rpb, lru_conv_w, lru_conv_b, lru_w_a, lru_b_a, lru_w_x, lru_b_x, lru_lambda, w_out, ffn_norm_pre, ffn_norm_post, ffn_w_in, ffn_w_out):
    batch, seq, d_model = x.shape
    assert d_model == D_MODEL and seq % TB_GLA == 0 and seq % (NA_BLOCK_ROWS * GRID_W) == 0
    assert all(seq % (d * DIL_STEP[d][0]) == 0 and d % DIL_STEP[d][1] == 0 for d in DILATIONS)
    depth = w_in.shape[0]
    n = batch * seq
    xf = x.reshape(n, D_MODEL)
    cos_t, sin_t = _rope_tables(seq)
    masks = _band_masks()
    rows = lambda t: t[..., None, :]
    w = dict(
        mix_pre=rows(mix_norm_pre), mix_post=rows(mix_norm_post), w_in=_reorder_w_in(w_in),
        gla_wg=_gate_weights(gla_w_gate), gla_bg=rows(gla_b_gate), gla_norm=rows(gla_norm),
        nbr_bias=_nbr_bias(na_rpb),
        lru_conv_w=lru_conv_w, lru_conv_b=rows(lru_conv_b),
        lru_wa=_block_diag(lru_w_a).astype(BF16), lru_ba=rows(lru_b_a),
        lru_wx=_block_diag(lru_w_x).astype(BF16), lru_bx=rows(lru_b_x), lru_lam=rows(lru_lambda),
        w_out=w_out.astype(BF16), ffn_pre=rows(ffn_norm_pre), ffn_post=rows(ffn_norm_post),
        ffn_in=ffn_w_in.astype(BF16), ffn_out=ffn_w_out.astype(BF16),
    )

    for l in range(depth):
        pa, pb, d1, d4, d16 = _in_proj(xf, w, l, cos_t, sin_t, batch, seq)

        gla_dirs = _gla(pa, w, l, batch, seq)
        yb = _nbr(pb, w, l, batch, seq)
        lru_dirs = (_lru_dir(pa, w, l, batch, seq, False), _lru_dir(pa, w, l, batch, seq, True))
        o4, l4 = _dil_branch(d4, masks, batch, seq, 4)
        o16, l16 = _dil_branch(d16, masks, batch, seq, 16)
        yd = _dil_branch(d1, masks, batch, seq, 1, others=(o4, l4, o16, l16))

        xf = _tail(pa, gla_dirs, yb, lru_dirs, yd, xf, w, l)
    return xf.reshape(batch, seq, D_MODEL)
```

```python
import functools

import numpy as np
import jax
import jax.numpy as jnp
from jax import lax
from jax.experimental import pallas as pl
from jax.experimental.pallas import tpu as pltpu

F32 = jnp.float32
BF16 = jnp.bfloat16

D_MODEL = 1024
HEAD_DIM = 64
HEADS = 4
GW = HEADS * HEAD_DIM
GLA_RANK = 16
GLA_TAU = 16.0
GLA_CHUNK = 64
GRID_W = 64
NA_ROWS = 8
NA_COLS = 16
LRU_C = 8.0
LRU_CONV = 4
DILATIONS = (1, 4, 16)
DIL_RADIUS = 64
ROPE_THETA = 10000.0
D_FF = 2816
EPS = 1e-6
QK_SCALE = HEAD_DIM ** -0.5
LANES = 128

CB_QA, CB_KA, CB_VA, CB_GA, CB_XC, CB_GC = 0, 1, 2, 3, 4, 5
Z_COL0 = 6 * GW
PA_COLS = Z_COL0 + LANES
QKV = 3 * GW
P_COLS = PA_COLS + 2 * QKV

VMEM_LIMIT = 56 * 1024 * 1024


def _params(sem, vmem=VMEM_LIMIT):
    return pltpu.CompilerParams(dimension_semantics=sem, vmem_limit_bytes=vmem)


def _stack_heads(t):
    rows = t.shape[0]
    lane = lax.broadcasted_iota(jnp.int32, (rows, LANES), 1)
    zero = jnp.zeros((rows, LANES), t.dtype)
    blocks = []
    for h in range(HEADS):
        half = t[:, (h // 2) * LANES:(h // 2 + 1) * LANES]
        half = jnp.where((lane < HEAD_DIM) if h % 2 == 0 else (lane >= HEAD_DIM), half, zero)
        blocks.append(jnp.concatenate([half, zero] if h < 2 else [zero, half], axis=1))
    return jnp.concatenate(blocks, axis=0)


def _unstack_heads(t):
    rows = t.shape[0] // HEADS
    first = lax.broadcasted_iota(jnp.int32, (rows, LANES), 1) < HEAD_DIM
    blk = lambda h, tile: t[h * rows:(h + 1) * rows, tile * LANES:(tile + 1) * LANES]
    return jnp.concatenate([jnp.where(first, blk(0, 0), blk(1, 0)), jnp.where(first, blk(2, 1), blk(3, 1))], axis=1)


def _dot(a, b):
    return jnp.dot(a, b, preferred_element_type=F32)


def _dot_nt(a, b):
    return lax.dot_general(a, b, (((1,), (1,)), ((), ())), preferred_element_type=F32)


def _dot_tn(a, b):
    return lax.dot_general(a, b, (((0,), (0,)), ((), ())), preferred_element_type=F32)


def _split(t):
    hi = t.astype(BF16)
    return hi, (t - hi.astype(F32)).astype(BF16)


def _split_dot(m, t):
    hi, lo = _split(t)
    return _dot(m, hi) + _dot(m, lo)


def _split_dot_rhs(t, m):
    hi, lo = _split(t)
    return _dot(hi, m) + _dot(lo, m)


def _rms(x, g):
    ms = jnp.mean(x * x, axis=-1, keepdims=True)
    return x * lax.rsqrt(ms + EPS) * g


def _sigmoid(x):
    return 1.0 / (1.0 + jnp.exp(-x))


def _softplus(x):
    return jnp.maximum(x, 0.0) + jnp.log1p(jnp.exp(-jnp.abs(x)))


def _softmax_pv(scores, values):
    maxes = [jnp.max(s, axis=-1, keepdims=True) for s in scores]
    exps = [jnp.exp(s - m) for s, m in zip(scores, maxes)]
    sums = [jnp.sum(e, axis=-1, keepdims=True) for e in exps]
    pvs = [_dot(e.astype(BF16), v) for e, v in zip(exps, values)]
    out = []
    for pv, m, d in zip(pvs, maxes, sums):
        den = _unstack_heads(jnp.broadcast_to(d, pv.shape))
        o = _unstack_heads(pv) / den
        lse = _unstack_heads(jnp.broadcast_to(m, pv.shape)) + jnp.log(den)
        out.append((o, lse))
    return out


TM_IN = 512


def _rope(t, c, s):
    lane = lax.broadcasted_iota(jnp.int32, t.shape, 1)
    first_half = (lane & (HEAD_DIM - 1)) < HEAD_DIM // 2
    swapped = jnp.where(first_half, pltpu.roll(t, GW - HEAD_DIM // 2, 1), pltpu.roll(t, HEAD_DIM // 2, 1))
    return t * c + swapped * s


def _in_proj_kernel(x_ref, g_ref, w_ref, c_ref, s_ref, pa_ref, pb_ref, d1_ref, d4_ref, d16_ref, rope_ref):
    h = _rms(x_ref[...], g_ref[...]).astype(BF16)
    pd = _dot(h, w_ref[:, PA_COLS + QKV:P_COLS])
    pb = _dot(h, w_ref[:, PA_COLS:PA_COLS + QKV])
    pb_ref[:, 0:GW] = (pb[:, 0:GW] * QK_SCALE).astype(BF16)
    pb_ref[:, GW:QKV] = pb[:, GW:QKV].astype(BF16)
    c = c_ref[...]
    s = s_ref[...]
    pa_ref[...] = _dot(h, w_ref[:, 0:PA_COLS])
    qkv = jnp.concatenate([_rope(pd[:, 0:GW], c, s) * QK_SCALE, _rope(pd[:, GW:2 * GW], c, s),
                           pd[:, 2 * GW:QKV]], axis=1)
    d1_ref[0] = qkv.astype(BF16)
    for t in range(QKV // LANES):
        lanes = slice(t * LANES, (t + 1) * LANES)
        rope_ref[t] = qkv[:, lanes]
        for d, ref in ((4, d4_ref), (16, d16_ref)):
            for r in range(d):
                ref[r, :, lanes] = rope_ref[t, pl.ds(r, TM_IN // d, stride=d), :].astype(BF16)


def _in_proj(x, w, l, cos_t, sin_t, batch, seq):
    n = x.shape[0]
    nl = seq // TM_IN
    row = lambda cols: pl.BlockSpec((TM_IN, cols), lambda i: (i, 0))
    tab = pl.BlockSpec((TM_IN, GW), lambda i: (i % nl, 0))
    res = lambda d: pl.BlockSpec((None, d, TM_IN // d, QKV), lambda i: (i // nl, 0, i % nl, 0))
    return pl.pallas_call(
        _in_proj_kernel,
        grid=(n // TM_IN,),
        in_specs=[row(D_MODEL),
                  pl.BlockSpec((None, 1, D_MODEL), lambda i: (l, 0, 0)),
                  pl.BlockSpec((None, D_MODEL, P_COLS), lambda i: (l, 0, 0)),
                  tab, tab],
        out_specs=[row(PA_COLS), row(QKV), res(1), res(4), res(16)],
        out_shape=[jax.ShapeDtypeStruct((n, PA_COLS), F32),
                   jax.ShapeDtypeStruct((n, QKV), BF16),
                   jax.ShapeDtypeStruct((batch, 1, seq, QKV), BF16),
                   jax.ShapeDtypeStruct((batch, 4, seq // 4, QKV), BF16),
                   jax.ShapeDtypeStruct((batch, 16, seq // 16, QKV), BF16)],
        scratch_shapes=[pltpu.VMEM((QKV // LANES, TM_IN, LANES), F32)],
        compiler_params=_params(("parallel",)),
        name="in_proj",
    )(x, w["mix_pre"], w["w_in"], cos_t, sin_t)


TM_FFN = 512
MXU_TILE = 256
FF_CHUNKS = ((0, 6 * MXU_TILE), (6 * MXU_TILE, D_FF))


def _tail_kernel(af_ref, ab_ref, ag_ref, an_ref, yb_ref, cf_ref, cb_ref, cg_ref, yd_ref,
                 wm_ref, x_ref, gm_ref, g1_ref, wi_ref, wo_ref, g2_ref, o_ref):
    ya = _gla_finalize(af_ref[...] + ab_ref[...], ag_ref[...], an_ref[...])
    yc = (cf_ref[...] + cb_ref[...]) * _gelu_tanh(cg_ref[...])
    y = _dot(ya.astype(BF16), wm_ref[0 * GW:1 * GW, :])
    y += _dot(yb_ref[...].astype(BF16), wm_ref[1 * GW:2 * GW, :])
    y += _dot(yc.astype(BF16), wm_ref[2 * GW:3 * GW, :])
    y += _dot(yd_ref[...].astype(BF16), wm_ref[3 * GW:4 * GW, :])
    x1 = x_ref[...] + _rms(y, gm_ref[...])
    h = _rms(x1, g1_ref[...]).astype(BF16)
    f = None
    for lo, hi in FF_CHUNKS:
        gate = _dot(h, wi_ref[:, lo:hi])
        up = _dot(h, wi_ref[:, D_FF + lo:D_FF + hi])
        part = _dot((gate * _sigmoid(gate) * up).astype(BF16), wo_ref[lo:hi, :])
        f = part if f is None else f + part
    o_ref[...] = x1 + _rms(f, g2_ref[...])


def _tail(pa, gla_dirs, yb, lru_dirs, yd, x, w, l):
    n = x.shape[0]
    row = pl.BlockSpec((TM_FFN, GW), lambda i: (i, 0))
    stream = lambda cb: pl.BlockSpec((TM_FFN, GW), lambda i: (i, cb))
    gain = pl.BlockSpec((None, 1, D_MODEL), lambda i: (l, 0, 0))
    resident = lambda r, c: pl.BlockSpec((None, r, c), lambda i: (l, 0, 0), pipeline_mode=pl.Buffered(1))
    return pl.pallas_call(
        _tail_kernel,
        grid=(n // TM_FFN,),
        in_specs=[row, row, stream(CB_GA), pl.BlockSpec((None, 1, GW), lambda i: (l, 0, 0)), row,
                  row, row, stream(CB_GC), row] + [
            resident(D_MODEL, D_MODEL),
            pl.BlockSpec((TM_FFN, D_MODEL), lambda i: (i, 0)),
            gain, gain,
            resident(D_MODEL, 2 * D_FF),
            resident(D_FF, D_MODEL),
            gain,
        ],
        out_specs=pl.BlockSpec((TM_FFN, D_MODEL), lambda i: (i, 0)),
        out_shape=jax.ShapeDtypeStruct((n, D_MODEL), F32),
        compiler_params=_params(("parallel",)),
        name="out_proj_ffn",
    )(gla_dirs[0], gla_dirs[1], pa, w["gla_norm"], yb, lru_dirs[0], lru_dirs[1], pa, yd,
      w["w_out"], x, w["mix_post"], w["ffn_pre"], w["ffn_in"], w["ffn_out"], w["ffn_post"])


TB_GLA = 512


def _same_head_tile():
    rb = lax.broadcasted_iota(jnp.int32, (LANES, LANES), 0) >> 6
    cb = lax.broadcasted_iota(jnp.int32, (LANES, LANES), 1) >> 6
    return rb == cb


def _gla_direction(q_ref, k_ref, v_ref, z_ref, wg_ref, bg_ref, o_ref, st_ref, reverse):
    C = GLA_CHUNK
    n_chunks = TB_GLA // C
    logit = _dot(z_ref[...].astype(BF16), wg_ref[...]) + bg_ref[...]
    log_a = -_softplus(-logit) * (1.0 / GLA_TAU)

    r_i = lax.broadcasted_iota(jnp.int32, (C, C), 0)
    c_i = lax.broadcasted_iota(jnp.int32, (C, C), 1)
    cum = jnp.where((c_i >= r_i) if reverse else (c_i <= r_i), 1.0, 0.0).astype(BF16)
    t_i = lax.broadcasted_iota(jnp.int32, (C, LANES), 0)
    s_i = lax.broadcasted_iota(jnp.int32, (C, LANES), 1) & (C - 1)
    causal = (s_i >= t_i) if reverse else (s_i <= t_i)
    same_head = _same_head_tile()
    last, mid = (0, C // 2) if reverse else (C - 1, C // 2 - 1)

    def stack2(t):
        lane = lax.broadcasted_iota(jnp.int32, t.shape, 1)
        zero = jnp.zeros_like(t)
        return jnp.concatenate([jnp.where(lane < HEAD_DIM, t, zero), jnp.where(lane >= HEAD_DIM, t, zero)], axis=0)

    b_c = [_split_dot(cum, log_a[c * C:(c + 1) * C]) for c in range(n_chunks)]
    b = jnp.concatenate(b_c, axis=0)
    b_mid = jnp.concatenate([jnp.broadcast_to(t[mid:mid + 1], (C, GW)) for t in b_c], axis=0)
    q = q_ref[...] * QK_SCALE
    k = k_ref[...]
    v = v_ref[...].astype(BF16)
    q_mid = q * jnp.exp(b - b_mid)
    k_mid = k * jnp.exp(b_mid - b)
    q_in = q_mid.astype(BF16)
    k_in = k_mid.astype(BF16)
    q_st = jnp.concatenate([q_mid[c * C:(c + 1) * C] * jnp.exp(t[mid:mid + 1]) for c, t in enumerate(b_c)],
                           axis=0).astype(BF16)
    k_st = jnp.concatenate([k_mid[c * C:(c + 1) * C] * jnp.exp(t[last:last + 1] - t[mid:mid + 1])
                            for c, t in enumerate(b_c)], axis=0).astype(BF16)

    yield

    chunk = [slice(c * C, (c + 1) * C) for c in range(n_chunks)]
    order = range(n_chunks - 1, -1, -1) if reverse else range(n_chunks)
    for tile in range(GW // LANES):
        lanes = slice(tile * LANES, (tile + 1) * LANES)
        att = [_dot_nt(q_in[r, lanes], stack2(k_in[r, lanes])) for r in chunk]
        yield
        o_in = [_dot(jnp.where(causal, a, 0.0).astype(BF16), stack2(v[r, lanes])) for a, r in zip(att, chunk)]
        upd = [_dot_tn(v[r, lanes], k_st[r, lanes]) for r in chunk]
        yield
        st = st_ref[tile]
        for c in order:
            o_ref[chunk[c], lanes] = o_in[c] + _dot_nt(q_st[chunk[c], lanes], st.astype(BF16))
            st = st * jnp.exp(b_c[c][last:last + 1, lanes]) + jnp.where(same_head, upd[c], 0.0)
            yield
        st_ref[tile] = st


def _gla_kernel(qf_ref, kf_ref, vf_ref, zf_ref, qb_ref, kb_ref, vb_ref, zb_ref, wg_ref, bg_ref,
                of_ref, ob_ref, stf_ref, stb_ref):
    @pl.when(pl.program_id(1) == 0)
    def _():
        stf_ref[...] = jnp.zeros_like(stf_ref)
        stb_ref[...] = jnp.zeros_like(stb_ref)

    stages = [_gla_direction(qf_ref, kf_ref, vf_ref, zf_ref, wg_ref.at[0], bg_ref.at[0], of_ref, stf_ref, False),
              _gla_direction(qb_ref, kb_ref, vb_ref, zb_ref, wg_ref.at[1], bg_ref.at[1], ob_ref, stb_ref, True)]
    while stages:
        stages = [s for s in stages if next(s, True) is None]


def _gla_finalize(o, g, norm_g):
    blk = jnp.where(_same_head_tile(), 1.0 / HEAD_DIM, 0.0).astype(BF16)
    sq = o * o
    ms = jnp.concatenate([_split_dot_rhs(sq[:, t * LANES:(t + 1) * LANES], blk) for t in range(GW // LANES)],
                         axis=1)
    return o * lax.rsqrt(ms + EPS) * norm_g * (g * _sigmoid(g))


def _gla(pa, w, l, batch, seq):
    nb = seq // TB_GLA
    n = batch * seq
    fwd = lambda b, i: b * nb + i
    bwd = lambda b, i: b * nb + nb - 1 - i
    col = lambda rb, cb: pl.BlockSpec((TB_GLA, GW), lambda b, i: (rb(b, i), cb))
    gate = lambda rb: pl.BlockSpec((TB_GLA, LANES), lambda b, i: (rb(b, i), Z_COL0 // LANES))
    out_sd = jax.ShapeDtypeStruct((n, GW), F32)
    state = pltpu.VMEM((GW // LANES, LANES, LANES), F32)
    return pl.pallas_call(
        _gla_kernel,
        grid=(batch, nb),
        in_specs=[col(fwd, CB_QA), col(fwd, CB_KA), col(fwd, CB_VA), gate(fwd),
                  col(bwd, CB_QA), col(bwd, CB_KA), col(bwd, CB_VA), gate(bwd),
                  pl.BlockSpec((None, 2, LANES, GW), lambda b, i: (l, 0, 0, 0)),
                  pl.BlockSpec((None, 2, 1, GW), lambda b, i: (l, 0, 0, 0))],
        out_specs=[col(fwd, 0), col(bwd, 0)],
        out_shape=[out_sd, out_sd],
        scratch_shapes=[state, state],
        compiler_params=_params(("arbitrary", "arbitrary")),
        name="gla",
    )(pa, pa, pa, pa, pa, pa, pa, pa, w["gla_wg"], w["gla_bg"])


NA_BLOCK_ROWS = 16
NA_GROUP = 16
NA_PAIRS = 2 * NA_ROWS - 2


def _nbr_kernel(q_ref, k_ref, v_ref, bias_ref, o_ref, *, grid_rows):
    W = GRID_W
    win = NA_ROWS * W
    row0 = pl.program_id(1) * NA_BLOCK_ROWS

    def body(g, carry):
        scores, starts = [], []
        for j in range(NA_GROUP):
            r = row0 + g * NA_GROUP + j
            start = jnp.clip(r - NA_ROWS // 2, 0, grid_rows - NA_ROWS)
            pat = start - r + (NA_ROWS - 1)
            k0 = pl.multiple_of(start * W, W)
            q0 = pl.multiple_of((g * NA_GROUP + j) * W, W)
            s = _dot_nt(_stack_heads(q_ref[pl.ds(q0, W), :]), k_ref[pl.ds(k0, win), :])
            scores.append(s + jnp.concatenate([bias_ref[pat + 2 * t] for t in range(NA_ROWS // 2)], axis=1))
            starts.append(k0)
        results = _softmax_pv(scores, [v_ref[pl.ds(k0, win), :] for k0 in starts])
        for j, (o, _) in enumerate(results):
            o_ref[pl.ds(pl.multiple_of((g * NA_GROUP + j) * W, W), W), :] = o
        return carry

    lax.fori_loop(0, NA_BLOCK_ROWS // NA_GROUP, body, 0)


def _nbr(pb, w, l, batch, seq):
    grid_rows = seq // GRID_W
    tq = NA_BLOCK_ROWS * GRID_W
    nb = seq // tq
    return pl.pallas_call(
        functools.partial(_nbr_kernel, grid_rows=grid_rows),
        grid=(batch, nb),
        in_specs=[
            pl.BlockSpec((tq, GW), lambda b, i: (b * nb + i, 0)),
            pl.BlockSpec((seq, GW), lambda b, i: (b, 1)),
            pl.BlockSpec((seq, GW), lambda b, i: (b, 2)),
            pl.BlockSpec((None, NA_PAIRS, HEADS * GRID_W, 2 * GRID_W), lambda b, i: (l, 0, 0, 0)),
        ],
        out_specs=pl.BlockSpec((tq, GW), lambda b, i: (b * nb + i, 0)),
        out_shape=jax.ShapeDtypeStruct((batch * seq, GW), F32),
        compiler_params=_params(("parallel", "arbitrary")),
        name="nbr_attn",
    )(pb, pb, pb, w["nbr_bias"])


def _nbr_bias(rpb):
    depth = rpb.shape[0]
    c = np.arange(GRID_W)
    dc = np.clip(c[None, :] - c[:, None], -(NA_COLS - 1), NA_COLS - 1) + NA_COLS - 1
    onehot = jnp.asarray(dc.reshape(-1)[None, :] == np.arange(2 * NA_COLS - 1)[:, None], F32)
    t = jnp.dot(rpb.reshape(-1, 2 * NA_COLS - 1), onehot, precision=lax.Precision.HIGHEST)
    t = t.reshape(depth, HEADS, 2 * NA_ROWS - 1, GRID_W, GRID_W)
    col_start = np.clip(c - NA_COLS // 2, 0, GRID_W - NA_COLS)
    col_ok = (c[None, :] >= col_start[:, None]) & (c[None, :] < col_start[:, None] + NA_COLS)
    t = jnp.where(jnp.asarray(col_ok), t, -jnp.inf)
    t = t.transpose(0, 2, 1, 3, 4).reshape(depth, 2 * NA_ROWS - 1, HEADS * GRID_W, GRID_W)
    return jnp.concatenate([t[:, :-1], t[:, 1:]], axis=-1)


TB_LRU = 512
SUB = 8


def _gelu_tanh(x):
    return 0.5 * x * (1.0 + jnp.tanh(0.7978845608028654 * (x + 0.044715 * x * x * x)))


def _lru_kernel(x_ref, xp_ref, xn_ref, cw_ref, cb_ref, wa_ref, ba_ref, wx_ref, bx_ref, lam_ref,
                o_ref, ext_ref, ae_ref, ue_ref, cin_ref, carry_ref, *, reverse):
    TB = TB_LRU
    i = pl.program_id(1)
    nb = pl.num_programs(1)
    seq_blk = (nb - 1 - i) if reverse else i

    @pl.when(i == 0)
    def _():
        carry_ref[...] = jnp.zeros_like(carry_ref)

    ext_ref[0:SUB, :] = jnp.where(seq_blk == 0, 0.0, xp_ref[...])
    ext_ref[SUB:SUB + TB, :] = x_ref[...]
    ext_ref[SUB + TB:, :] = jnp.where(seq_blk == nb - 1, 0.0, xn_ref[...])
    xc = cb_ref[...] + jnp.zeros((TB, GW), F32)
    for j in range(LRU_CONV):
        xc = xc + ext_ref[pl.ds(SUB - 2 + j, TB), :] * cw_ref[j:j + 1, :]

    xcb = xc.astype(BF16)
    r = _sigmoid(_dot(xcb, wa_ref[...]) + ba_ref[...])
    gi = _sigmoid(_dot(xcb, wx_ref[...]) + bx_ref[...])
    log_a = (-LRU_C) * r * _softplus(-lam_ref[...])
    a = jnp.exp(log_a)
    u = jnp.sqrt(-jnp.tanh(log_a) * (a * a + 1.0)) * (gi * xc)

    a = a.reshape(TB // SUB, SUB, GW)
    u = u.reshape(TB // SUB, SUB, GW)
    rm = lax.broadcasted_iota(jnp.int32, (TB // SUB, SUB, GW), 1)
    for d in (1, 2, 4):
        shift, ok = (SUB - d, rm < SUB - d) if reverse else (d, rm >= d)
        a_sh, u_sh = pltpu.roll(a, shift, 1), pltpu.roll(u, shift, 1)
        u = jnp.where(ok, a * u_sh + u, u)
        a = jnp.where(ok, a * a_sh, a)
    nt = TB // SUB
    edge = 0 if reverse else SUB - 1
    ae_ref[...] = jnp.broadcast_to(a[:, edge:edge + 1, :], a.shape).reshape(TB, GW)
    ue_ref[...] = jnp.broadcast_to(u[:, edge:edge + 1, :], u.shape).reshape(TB, GW)

    def body(t, carry):
        off = pl.multiple_of(((nt - 1 - t) if reverse else t) * SUB, SUB)
        cin_ref[pl.ds(off, SUB), :] = carry
        return ue_ref[pl.ds(off, SUB), :] + ae_ref[pl.ds(off, SUB), :] * carry

    carry_ref[...] = lax.fori_loop(0, nt, body, carry_ref[...], unroll=8)

    o_ref[...] = u.reshape(TB, GW) + a.reshape(TB, GW) * cin_ref[...]


def _lru_dir(pa, w, l, batch, seq, reverse):
    nb = seq // TB_LRU
    n = batch * seq
    tiles = TB_LRU // SUB
    e = 1 if reverse else 0

    def rb(b, i):
        return b * nb + ((nb - 1 - i) if reverse else i)

    per_layer = lambda rows: pl.BlockSpec((None, rows, GW), lambda b, i: (l, 0, 0))
    per_dir = lambda rows: pl.BlockSpec((None, None, rows, GW), lambda b, i: (l, e, 0, 0))
    in_specs = [
        pl.BlockSpec((TB_LRU, GW), lambda b, i: (rb(b, i), CB_XC)),
        pl.BlockSpec((SUB, GW), lambda b, i: (jnp.maximum(rb(b, i) * tiles - 1, 0), CB_XC)),
        pl.BlockSpec((SUB, GW), lambda b, i: (jnp.minimum((rb(b, i) + 1) * tiles, n // SUB - 1), CB_XC)),
        per_layer(LRU_CONV), per_layer(1), per_dir(GW), per_dir(1), per_dir(GW), per_dir(1), per_dir(1),
    ]
    args = [pa, pa, pa, w["lru_conv_w"], w["lru_conv_b"], w["lru_wa"], w["lru_ba"], w["lru_wx"], w["lru_bx"],
            w["lru_lam"]]
    return pl.pallas_call(
        functools.partial(_lru_kernel, reverse=reverse),
        grid=(batch, nb),
        in_specs=in_specs,
        out_specs=pl.BlockSpec((TB_LRU, GW), lambda b, i: (rb(b, i), 0)),
        out_shape=jax.ShapeDtypeStruct((n, GW), F32),
        scratch_shapes=[pltpu.VMEM((TB_LRU + 2 * SUB, GW), F32)] + [pltpu.VMEM((TB_LRU, GW), F32)] * 3
                       + [pltpu.VMEM((SUB, GW), F32)],
        compiler_params=_params(("arbitrary", "arbitrary")),
        name="lru_bwd" if reverse else "lru_fwd",
    )(*args)


SB_DIL = 128
GROUP_DIL = 2
WIN_DIL = SB_DIL + 2 * DIL_RADIUS
DIL_STEP = {1: (1024, 1), 4: (1024, 1), 16: (512, 2)}


def _band_masks():
    q = np.arange(HEADS * SB_DIL)[:, None] % SB_DIL
    c = np.arange(WIN_DIL)[None, :]
    offs = (0, -DIL_RADIUS, -2 * DIL_RADIUS)
    return jnp.asarray(np.stack([np.where(np.abs(c + off - q) <= DIL_RADIUS, 0.0, -np.inf) for off in offs]), F32)


def _dil_kernel(*refs, sub_len, tj, rps, merge):
    if merge:
        (q_ref, k_ref, v_ref, mask_ref, o4_ref, l4_ref, o16_ref, l16_ref, o_ref,
         n4o_ref, n4l_ref, n16o_ref, n16l_ref) = refs
        for d, src_o, src_l, dst_o, dst_l in ((4, o4_ref, l4_ref, n4o_ref, n4l_ref),
                                              (16, o16_ref, l16_ref, n16o_ref, n16l_ref)):
            for r in range(d):
                for t in range(GW // LANES):
                    lanes = slice(t * LANES, (t + 1) * LANES)
                    dst_o[t, pl.ds(r, tj // d, stride=d), :] = src_o[r, :, lanes]
                    dst_l[t, pl.ds(r, tj // d, stride=d), :] = src_l[r, :, lanes]
    else:
        q_ref, k_ref, v_ref, mask_ref, o_ref, l_ref = refs
    j_blk = pl.program_id(2) * tj
    blocks = [(rr, sb) for rr in range(rps) for sb in range(tj // SB_DIL)]
    for g in range(0, len(blocks), GROUP_DIL):
        group = blocks[g:g + GROUP_DIL]
        scores, values = [], []
        for rr, sb in group:
            rows = slice(sb * SB_DIL, (sb + 1) * SB_DIL)
            j0 = j_blk + sb * SB_DIL
            ws = pl.multiple_of(jnp.clip(j0 - DIL_RADIUS, 0, sub_len - WIN_DIL), DIL_RADIUS)
            which = 1 - (j0 == 0).astype(jnp.int32) + (j0 == sub_len - SB_DIL).astype(jnp.int32)
            s = _dot_nt(_stack_heads(q_ref[rr, rows, :]), k_ref[rr, pl.ds(ws, WIN_DIL), :])
            scores.append(s + mask_ref[which])
            values.append(v_ref[rr, pl.ds(ws, WIN_DIL), :])
        for (rr, sb), (o, lse) in zip(group, _softmax_pv(scores, values)):
            rows = slice(sb * SB_DIL, (sb + 1) * SB_DIL)
            if merge:
                tok = lambda ref: jnp.concatenate([ref[t, rows, :] for t in range(GW // LANES)], axis=1)
                l4, l16 = tok(n4l_ref), tok(n16l_ref)
                m = jnp.maximum(jnp.maximum(lse, l4), l16)
                e1, e4, e16 = jnp.exp(lse - m), jnp.exp(l4 - m), jnp.exp(l16 - m)
                o_ref[rows, :] = (e1 * o + e4 * tok(n4o_ref) + e16 * tok(n16o_ref)) / (e1 + e4 + e16)
            else:
                o_ref[rr, rows, :] = o
                l_ref[rr, rows, :] = lse


def _dil_branch(qkv, masks, batch, seq, dil, others=None):
    sub_len = seq // dil
    tj, rps = DIL_STEP[dil]
    nj = sub_len // tj
    col = lambda cb, rows, jmap: pl.BlockSpec((None, rps, rows, GW), lambda b, r, j: (b, r, jmap(j), cb))
    in_specs = [col(0, tj, lambda j: j), col(1, sub_len, lambda j: 0), col(2, sub_len, lambda j: 0),
                pl.BlockSpec(masks.shape, lambda b, r, j: (0, 0, 0))]
    out_blk = pl.BlockSpec((None, rps, tj, GW), lambda b, r, j: (b, r, j, 0))
    out_sd = jax.ShapeDtypeStruct((batch, dil, sub_len, GW), F32)
    args = [qkv, qkv, qkv, masks]
    scratch = []
    if others is None:
        out_specs, out_shape = [out_blk, out_blk], [out_sd, out_sd]
    else:
        assert dil == 1 and rps == 1
        for d in (4, 16):
            in_specs += [pl.BlockSpec((None, d, tj // d, GW), lambda b, r, j: (b, 0, j, 0))] * 2
        args += list(others)
        out_specs = pl.BlockSpec((tj, GW), lambda b, r, j: (b * nj + j, 0))
        out_shape = jax.ShapeDtypeStruct((batch * seq, GW), F32)
        scratch = [pltpu.VMEM((GW // LANES, tj, LANES), F32)] * 4
    return pl.pallas_call(
        functools.partial(_dil_kernel, sub_len=sub_len, tj=tj, rps=rps, merge=others is not None),
        grid=(batch, dil // rps, nj),
        in_specs=in_specs,
        out_specs=out_specs,
        out_shape=out_shape,
        scratch_shapes=scratch,
        compiler_params=_params(("parallel", "parallel", "arbitrary")),
        name=f"dil_attn_d{dil}",
    )(*args)


TK_WPREP = 256


def _reorder_w_in_kernel(w_ref, o_ref):
    za = 4 * GW
    qb = za + 2 * GLA_RANK
    xc = qb + QKV
    qd = xc + 2 * GW
    lane = lax.broadcasted_iota(jnp.int32, (TK_WPREP, LANES), 1)
    o_ref[:, 0:za] = w_ref[:, 0:za].astype(BF16)
    o_ref[:, za:Z_COL0] = w_ref[:, xc:qd].astype(BF16)
    o_ref[:, Z_COL0:PA_COLS] = jnp.where(lane < 2 * GLA_RANK, w_ref[:, za:za + LANES], 0.0).astype(BF16)
    o_ref[:, PA_COLS:PA_COLS + QKV] = w_ref[:, qb:xc].astype(BF16)
    o_ref[:, PA_COLS + QKV:P_COLS] = w_ref[:, qd:qd + QKV].astype(BF16)


def _reorder_w_in(w_in):
    depth, rows, cols = w_in.shape
    return pl.pallas_call(
        _reorder_w_in_kernel,
        grid=(depth, rows // TK_WPREP),
        in_specs=[pl.BlockSpec((None, TK_WPREP, cols), lambda l, i: (l, i, 0))],
        out_specs=pl.BlockSpec((None, TK_WPREP, P_COLS), lambda l, i: (l, i, 0)),
        out_shape=jax.ShapeDtypeStruct((depth, rows, P_COLS), BF16),
        compiler_params=_params(("parallel", "parallel")),
        name="w_in_layout",
    )(w_in)


def _block_diag(w):
    eye = jnp.eye(HEADS, dtype=w.dtype)[:, None, :, None]
    out = w[..., :, :, None, :] * eye
    return out.reshape(w.shape[:-3] + (GW, GW))


def _gate_weights(w_gate):
    per_dir = [jnp.pad(w_gate[:, e], ((0, 0), (e * GLA_RANK, LANES - (e + 1) * GLA_RANK), (0, 0))) for e in (0, 1)]
    return jnp.stack(per_dir, axis=1).astype(BF16)


def _rope_tables(seq):
    pos = jnp.arange(seq, dtype=F32)
    inv_freq = ROPE_THETA ** (-jnp.arange(0, HEAD_DIM, 2, dtype=F32) / HEAD_DIM)
    ang = pos[:, None] * inv_freq[None, :]
    cos, sin = jnp.cos(ang), jnp.sin(ang)
    return jnp.tile(jnp.concatenate([cos, cos], -1), (1, HEADS)), jnp.tile(jnp.concatenate([-sin, sin], -1), (1, HEADS))


def kernel(x, mix_norm_pre, mix_norm_post, w_in, gla_w_gate, gla_b_gate, gla_norm, na_rpb, lru_conv_w, lru_conv_b, lru_w_a, lru_b_a, lru_w_x, lru_b_x, lru_lambda, w_out, ffn_norm_pre, ffn_norm_post, ffn_w_in, ffn_w_out):
    batch, seq, d_model = x.shape
    assert d_model == D_MODEL and seq % TB_GLA == 0 and seq % (NA_BLOCK_ROWS * GRID_W) == 0
    assert all(seq % (d * DIL_STEP[d][0]) == 0 and d % DIL_STEP[d][1] == 0 for d in DILATIONS)
    depth = w_in.shape[0]
    n = batch * seq
    xf = x.reshape(n, D_MODEL)
    cos_t, sin_t = _rope_tables(seq)
    masks = _band_masks()
    rows = lambda t: t[..., None, :]
    w = dict(
        mix_pre=rows(mix_norm_pre), mix_post=rows(mix_norm_post), w_in=_reorder_w_in(w_in),
        gla_wg=_gate_weights(gla_w_gate), gla_bg=rows(gla_b_gate), gla_norm=rows(gla_norm),
        nbr_bias=_nbr_bias(na_rpb),
        lru_conv_w=lru_conv_w, lru_conv_b=rows(lru_conv_b),
        lru_wa=_block_diag(lru_w_a).astype(BF16), lru_ba=rows(lru_b_a),
        lru_wx=_block_diag(lru_w_x).astype(BF16), lru_bx=rows(lru_b_x), lru_lam=rows(lru_lambda),
        w_out=w_out.astype(BF16), ffn_pre=rows(ffn_norm_pre), ffn_post=rows(ffn_norm_post),
        ffn_in=ffn_w_in.astype(BF16), ffn_out=ffn_w_out.astype(BF16),
    )

    for l in range(depth):
        pa, pb, d1, d4, d16 = _in_proj(xf, w, l, cos_t, sin_t, batch, seq)

        gla_dirs = _gla(pa, w, l, batch, seq)
        yb = _nbr(pb, w, l, batch, seq)
        lru_dirs = (_lru_dir(pa, w, l, batch, seq, False), _lru_dir(pa, w, l, batch, seq, True))
        o4, l4 = _dil_branch(d4, masks, batch, seq, 4)
        o16, l16 = _dil_branch(d16, masks, batch, seq, 16)
        yd = _dil_branch(d1, masks, batch, seq, 1, others=(o4, l4, o16, l16))

        xf = _tail(pa, gla_dirs, yb, lru_dirs, yd, xf, w, l)
    return xf.reshape(batch, seq, D_MODEL)
```

```python
import functools

import numpy as np
import jax
import jax.numpy as jnp
from jax import lax
from jax.experimental import pallas as pl
from jax.experimental.pallas import tpu as pltpu

F32 = jnp.float32
BF16 = jnp.bfloat16

D_MODEL = 1024
HEAD_DIM = 64
HEADS = 4
GW = HEADS * HEAD_DIM
GLA_RANK = 16
GLA_TAU = 16.0
GLA_CHUNK = 64
GRID_W = 64
NA_ROWS = 8
NA_COLS = 16
LRU_C = 8.0
LRU_CONV = 4
DILATIONS = (1, 4, 16)
DIL_RADIUS = 64
ROPE_THETA = 10000.0
D_FF = 2816
EPS = 1e-6
QK_SCALE = HEAD_DIM ** -0.5
LANES = 128

CB_QA, CB_KA, CB_VA, CB_GA, CB_XC, CB_GC = 0, 1, 2, 3, 4, 5
Z_COL0 = 6 * GW
PA_COLS = Z_COL0 + LANES
QKV = 3 * GW
P_COLS = PA_COLS + 2 * QKV

VMEM_LIMIT = 56 * 1024 * 1024


def _params(sem, vmem=VMEM_LIMIT):
    return pltpu.CompilerParams(dimension_semantics=sem, vmem_limit_bytes=vmem)


def _stack_heads(t):
    rows = t.shape[0]
    lane = lax.broadcasted_iota(jnp.int32, (rows, LANES), 1)
    zero = jnp.zeros((rows, LANES), t.dtype)
    blocks = []
    for h in range(HEADS):
        half = t[:, (h // 2) * LANES:(h // 2 + 1) * LANES]
        half = jnp.where((lane < HEAD_DIM) if h % 2 == 0 else (lane >= HEAD_DIM), half, zero)
        blocks.append(jnp.concatenate([half, zero] if h < 2 else [zero, half], axis=1))
    return jnp.concatenate(blocks, axis=0)


def _unstack_heads(t):
    rows = t.shape[0] // HEADS
    first = lax.broadcasted_iota(jnp.int32, (rows, LANES), 1) < HEAD_DIM
    blk = lambda h, tile: t[h * rows:(h + 1) * rows, tile * LANES:(tile + 1) * LANES]
    return jnp.concatenate([jnp.where(first, blk(0, 0), blk(1, 0)), jnp.where(first, blk(2, 1), blk(3, 1))], axis=1)


def _dot(a, b):
    return jnp.dot(a, b, preferred_element_type=F32)


def _dot_nt(a, b):
    return lax.dot_general(a, b, (((1,), (1,)), ((), ())), preferred_element_type=F32)


def _dot_tn(a, b):
    return lax.dot_general(a, b, (((0,), (0,)), ((), ())), preferred_element_type=F32)


def _split(t):
    hi = t.astype(BF16)
    return hi, (t - hi.astype(F32)).astype(BF16)


def _split_dot(m, t):
    hi, lo = _split(t)
    return _dot(m, hi) + _dot(m, lo)


def _split_dot_rhs(t, m):
    hi, lo = _split(t)
    return _dot(hi, m) + _dot(lo, m)


def _rms(x, g):
    ms = jnp.mean(x * x, axis=-1, keepdims=True)
    return x * lax.rsqrt(ms + EPS) * g


def _sigmoid(x):
    return 1.0 / (1.0 + jnp.exp(-x))


def _softplus(x):
    return jnp.maximum(x, 0.0) + jnp.log1p(jnp.exp(-jnp.abs(x)))


def _softmax_pv(scores, values):
    maxes = [jnp.max(s, axis=-1, keepdims=True) for s in scores]
    exps = [jnp.exp(s - m) for s, m in zip(scores, maxes)]
    sums = [jnp.sum(e, axis=-1, keepdims=True) for e in exps]
    pvs = [_dot(e.astype(BF16), v) for e, v in zip(exps, values)]
    out = []
    for pv, m, d in zip(pvs, maxes, sums):
        den = _unstack_heads(jnp.broadcast_to(d, pv.shape))
        o = _unstack_heads(pv) / den
        lse = _unstack_heads(jnp.broadcast_to(m, pv.shape)) + jnp.log(den)
        out.append((o, lse))
    return out


TM_IN = 512


def _rope(t, c, s):
    lane = lax.broadcasted_iota(jnp.int32, t.shape, 1)
    first_half = (lane & (HEAD_DIM - 1)) < HEAD_DIM // 2
    swapped = jnp.where(first_half, pltpu.roll(t, GW - HEAD_DIM // 2, 1), pltpu.roll(t, HEAD_DIM // 2, 1))
    return t * c + swapped * s


def _in_proj_kernel(x_ref, g_ref, w_ref, c_ref, s_ref, pa_ref, pb_ref, d1_ref, d4_ref, d16_ref, rope_ref):
    h = _rms(x_ref[...], g_ref[...]).astype(BF16)
    pd = _dot_nt(h, w_ref[PA_COLS + QKV:P_COLS, :])
    pb = _dot_nt(h, w_ref[PA_COLS:PA_COLS + QKV, :])
    pb_ref[:, 0:GW] = (pb[:, 0:GW] * QK_SCALE).astype(BF16)
    pb_ref[:, GW:QKV] = pb[:, GW:QKV].astype(BF16)
    c = c_ref[...]
    s = s_ref[...]
    pa_ref[...] = _dot_nt(h, w_ref[0:PA_COLS, :])
    qkv = jnp.concatenate([_rope(pd[:, 0:GW], c, s) * QK_SCALE, _rope(pd[:, GW:2 * GW], c, s),
                           pd[:, 2 * GW:QKV]], axis=1)
    d1_ref[0] = qkv.astype(BF16)
    for t in range(QKV // LANES):
        lanes = slice(t * LANES, (t + 1) * LANES)
        rope_ref[t] = qkv[:, lanes]
        for d, ref in ((4, d4_ref), (16, d16_ref)):
            for r in range(d):
                ref[r, :, lanes] = rope_ref[t, pl.ds(r, TM_IN // d, stride=d), :].astype(BF16)


def _in_proj(x, w, l, cos_t, sin_t, batch, seq):
    n = x.shape[0]
    nl = seq // TM_IN
    row = lambda cols: pl.BlockSpec((TM_IN, cols), lambda i: (i, 0))
    tab = pl.BlockSpec((TM_IN, GW), lambda i: (i % nl, 0))
    res = lambda d: pl.BlockSpec((None, d, TM_IN // d, QKV), lambda i: (i // nl, 0, i % nl, 0))
    return pl.pallas_call(
        _in_proj_kernel,
        grid=(n // TM_IN,),
        in_specs=[row(D_MODEL),
                  pl.BlockSpec((None, 1, D_MODEL), lambda i: (l, 0, 0)),
                  pl.BlockSpec((None, P_COLS, D_MODEL), lambda i: (l, 0, 0)),
                  tab, tab],
        out_specs=[row(PA_COLS), row(QKV), res(1), res(4), res(16)],
        out_shape=[jax.ShapeDtypeStruct((n, PA_COLS), F32),
                   jax.ShapeDtypeStruct((n, QKV), BF16),
                   jax.ShapeDtypeStruct((batch, 1, seq, QKV), BF16),
                   jax.ShapeDtypeStruct((batch, 4, seq // 4, QKV), BF16),
                   jax.ShapeDtypeStruct((batch, 16, seq // 16, QKV), BF16)],
        scratch_shapes=[pltpu.VMEM((QKV // LANES, TM_IN, LANES), F32)],
        compiler_params=_params(("parallel",)),
        name="in_proj",
    )(x, w["mix_pre"], w["w_in"], cos_t, sin_t)


TM_FFN = 512
MXU_TILE = 256
FF_CHUNKS = ((0, 6 * MXU_TILE), (6 * MXU_TILE, D_FF))


def _tail_kernel(af_ref, ab_ref, ag_ref, an_ref, yb_ref, cf_ref, cb_ref, cg_ref, yd_ref,
                 wm_ref, x_ref, gm_ref, g1_ref, wi_ref, wo_ref, g2_ref, o_ref):
    ya = _gla_finalize(af_ref[...] + ab_ref[...], ag_ref[...], an_ref[...])
    yc = (cf_ref[...] + cb_ref[...]) * _gelu_tanh(cg_ref[...])
    y = _dot(ya.astype(BF16), wm_ref[0 * GW:1 * GW, :])
    y += _dot(yb_ref[...].astype(BF16), wm_ref[1 * GW:2 * GW, :])
    y += _dot(yc.astype(BF16), wm_ref[2 * GW:3 * GW, :])
    y += _dot(yd_ref[...].astype(BF16), wm_ref[3 * GW:4 * GW, :])
    x1 = x_ref[...] + _rms(y, gm_ref[...])
    h = _rms(x1, g1_ref[...]).astype(BF16)
    f = None
    for lo, hi in FF_CHUNKS:
        gate = _dot(h, wi_ref[:, lo:hi])
        up = _dot(h, wi_ref[:, D_FF + lo:D_FF + hi])
        part = _dot((gate * _sigmoid(gate) * up).astype(BF16), wo_ref[lo:hi, :])
        f = part if f is None else f + part
    o_ref[...] = x1 + _rms(f, g2_ref[...])


def _tail(pa, gla_dirs, yb, lru_dirs, yd, x, w, l):
    n = x.shape[0]
    row = pl.BlockSpec((TM_FFN, GW), lambda i: (i, 0))
    stream = lambda cb: pl.BlockSpec((TM_FFN, GW), lambda i: (i, cb))
    gain = pl.BlockSpec((None, 1, D_MODEL), lambda i: (l, 0, 0))
    resident = lambda r, c: pl.BlockSpec((None, r, c), lambda i: (l, 0, 0), pipeline_mode=pl.Buffered(1))
    return pl.pallas_call(
        _tail_kernel,
        grid=(n // TM_FFN,),
        in_specs=[row, row, stream(CB_GA), pl.BlockSpec((None, 1, GW), lambda i: (l, 0, 0)), row,
                  row, row, stream(CB_GC), row] + [
            resident(D_MODEL, D_MODEL),
            pl.BlockSpec((TM_FFN, D_MODEL), lambda i: (i, 0)),
            gain, gain,
            resident(D_MODEL, 2 * D_FF),
            resident(D_FF, D_MODEL),
            gain,
        ],
        out_specs=pl.BlockSpec((TM_FFN, D_MODEL), lambda i: (i, 0)),
        out_shape=jax.ShapeDtypeStruct((n, D_MODEL), F32),
        compiler_params=_params(("parallel",)),
        name="out_proj_ffn",
    )(gla_dirs[0], gla_dirs[1], pa, w["gla_norm"], yb, lru_dirs[0], lru_dirs[1], pa, yd,
      w["w_out"], x, w["mix_post"], w["ffn_pre"], w["ffn_in"], w["ffn_out"], w["ffn_post"])


TB_GLA = 512


def _same_head_tile():
    rb = lax.broadcasted_iota(jnp.int32, (LANES, LANES), 0) >> 6
    cb = lax.broadcasted_iota(jnp.int32, (LANES, LANES), 1) >> 6
    return rb == cb


def _gla_direction(q_ref, k_ref, v_ref, z_ref, wg_ref, bg_ref, o_ref, st_ref, reverse):
    C = GLA_CHUNK
    n_chunks = TB_GLA // C
    logit = _dot(z_ref[...].astype(BF16), wg_ref[...]) + bg_ref[...]
    log_a = -_softplus(-logit) * (1.0 / GLA_TAU)

    r_i = lax.broadcasted_iota(jnp.int32, (C, C), 0)
    c_i = lax.broadcasted_iota(jnp.int32, (C, C), 1)
    cum = jnp.where((c_i >= r_i) if reverse else (c_i <= r_i), 1.0, 0.0).astype(BF16)
    t_i = lax.broadcasted_iota(jnp.int32, (C, LANES), 0)
    s_i = lax.broadcasted_iota(jnp.int32, (C, LANES), 1) & (C - 1)
    causal = (s_i >= t_i) if reverse else (s_i <= t_i)
    same_head = _same_head_tile()
    last, mid = (0, C // 2) if reverse else (C - 1, C // 2 - 1)

    def stack2(t):
        lane = lax.broadcasted_iota(jnp.int32, t.shape, 1)
        zero = jnp.zeros_like(t)
        return jnp.concatenate([jnp.where(lane < HEAD_DIM, t, zero), jnp.where(lane >= HEAD_DIM, t, zero)], axis=0)

    b_c = [_split_dot(cum, log_a[c * C:(c + 1) * C]) for c in range(n_chunks)]
    b = jnp.concatenate(b_c, axis=0)
    b_mid = jnp.concatenate([jnp.broadcast_to(t[mid:mid + 1], (C, GW)) for t in b_c], axis=0)
    q = q_ref[...] * QK_SCALE
    k = k_ref[...]
    v = v_ref[...].astype(BF16)
    q_mid = q * jnp.exp(b - b_mid)
    k_mid = k * jnp.exp(b_mid - b)
    q_in = q_mid.astype(BF16)
    k_in = k_mid.astype(BF16)
    q_st = jnp.concatenate([q_mid[c * C:(c + 1) * C] * jnp.exp(t[mid:mid + 1]) for c, t in enumerate(b_c)],
                           axis=0).astype(BF16)
    k_st = jnp.concatenate([k_mid[c * C:(c + 1) * C] * jnp.exp(t[last:last + 1] - t[mid:mid + 1])
                            for c, t in enumerate(b_c)], axis=0).astype(BF16)

    yield

    chunk = [slice(c * C, (c + 1) * C) for c in range(n_chunks)]
    order = range(n_chunks - 1, -1, -1) if reverse else range(n_chunks)
    for tile in range(GW // LANES):
        lanes = slice(tile * LANES, (tile + 1) * LANES)
        att = [_dot_nt(q_in[r, lanes], stack2(k_in[r, lanes])) for r in chunk]
        yield
        o_in = [_dot(jnp.where(causal, a, 0.0).astype(BF16), stack2(v[r, lanes])) for a, r in zip(att, chunk)]
        upd = [_dot_tn(v[r, lanes], k_st[r, lanes]) for r in chunk]
        yield
        st = st_ref[tile]
        for c in order:
            o_ref[chunk[c], lanes] = o_in[c] + _dot_nt(q_st[chunk[c], lanes], st.astype(BF16))
            st = st * jnp.exp(b_c[c][last:last + 1, lanes]) + jnp.where(same_head, upd[c], 0.0)
            yield
        st_ref[tile] = st


def _gla_kernel(qf_ref, kf_ref, vf_ref, zf_ref, qb_ref, kb_ref, vb_ref, zb_ref, wg_ref, bg_ref,
                of_ref, ob_ref, stf_ref, stb_ref):
    @pl.when(pl.program_id(1) == 0)
    def _():
        stf_ref[...] = jnp.zeros_like(stf_ref)
        stb_ref[...] = jnp.zeros_like(stb_ref)

    stages = [_gla_direction(qf_ref, kf_ref, vf_ref, zf_ref, wg_ref.at[0], bg_ref.at[0], of_ref, stf_ref, False),
              _gla_direction(qb_ref, kb_ref, vb_ref, zb_ref, wg_ref.at[1], bg_ref.at[1], ob_ref, stb_ref, True)]
    while stages:
        stages = [s for s in stages if next(s, True) is None]


def _gla_finalize(o, g, norm_g):
    blk = jnp.where(_same_head_tile(), 1.0 / HEAD_DIM, 0.0).astype(BF16)
    sq = o * o
    ms = jnp.concatenate([_split_dot_rhs(sq[:, t * LANES:(t + 1) * LANES], blk) for t in range(GW // LANES)],
                         axis=1)
    return o * lax.rsqrt(ms + EPS) * norm_g * (g * _sigmoid(g))


def _gla(pa, w, l, batch, seq):
    nb = seq // TB_GLA
    n = batch * seq
    fwd = lambda b, i: b * nb + i
    bwd = lambda b, i: b * nb + nb - 1 - i
    col = lambda rb, cb: pl.BlockSpec((TB_GLA, GW), lambda b, i: (rb(b, i), cb))
    gate = lambda rb: pl.BlockSpec((TB_GLA, LANES), lambda b, i: (rb(b, i), Z_COL0 // LANES))
    out_sd = jax.ShapeDtypeStruct((n, GW), F32)
    state = pltpu.VMEM((GW // LANES, LANES, LANES), F32)
    return pl.pallas_call(
        _gla_kernel,
        grid=(batch, nb),
        in_specs=[col(fwd, CB_QA), col(fwd, CB_KA), col(fwd, CB_VA), gate(fwd),
                  col(bwd, CB_QA), col(bwd, CB_KA), col(bwd, CB_VA), gate(bwd),
                  pl.BlockSpec((None, 2, LANES, GW), lambda b, i: (l, 0, 0, 0)),
                  pl.BlockSpec((None, 2, 1, GW), lambda b, i: (l, 0, 0, 0))],
        out_specs=[col(fwd, 0), col(bwd, 0)],
        out_shape=[out_sd, out_sd],
        scratch_shapes=[state, state],
        compiler_params=_params(("arbitrary", "arbitrary")),
        name="gla",
    )(pa, pa, pa, pa, pa, pa, pa, pa, w["gla_wg"], w["gla_bg"])


NA_BLOCK_ROWS = 16
NA_GROUP = 16
NA_PAIRS = 2 * NA_ROWS - 2


def _nbr_kernel(q_ref, k_ref, v_ref, bias_ref, o_ref, *, grid_rows):
    W = GRID_W
    win = NA_ROWS * W
    row0 = pl.program_id(1) * NA_BLOCK_ROWS

    def body(g, carry):
        scores, starts = [], []
        for j in range(NA_GROUP):
            r = row0 + g * NA_GROUP + j
            start = jnp.clip(r - NA_ROWS // 2, 0, grid_rows - NA_ROWS)
            pat = start - r + (NA_ROWS - 1)
            k0 = pl.multiple_of(start * W, W)
            q0 = pl.multiple_of((g * NA_GROUP + j) * W, W)
            s = _dot_nt(_stack_heads(q_ref[pl.ds(q0, W), :]), k_ref[pl.ds(k0, win), :])
            scores.append(s + jnp.concatenate([bias_ref[pat + 2 * t] for t in range(NA_ROWS // 2)], axis=1))
            starts.append(k0)
        results = _softmax_pv(scores, [v_ref[pl.ds(k0, win), :] for k0 in starts])
        for j, (o, _) in enumerate(results):
            o_ref[pl.ds(pl.multiple_of((g * NA_GROUP + j) * W, W), W), :] = o
        return carry

    lax.fori_loop(0, NA_BLOCK_ROWS // NA_GROUP, body, 0)


def _nbr(pb, w, l, batch, seq):
    grid_rows = seq // GRID_W
    tq = NA_BLOCK_ROWS * GRID_W
    nb = seq // tq
    return pl.pallas_call(
        functools.partial(_nbr_kernel, grid_rows=grid_rows),
        grid=(batch, nb),
        in_specs=[
            pl.BlockSpec((tq, GW), lambda b, i: (b * nb + i, 0)),
            pl.BlockSpec((seq, GW), lambda b, i: (b, 1)),
            pl.BlockSpec((seq, GW), lambda b, i: (b, 2)),
            pl.BlockSpec((None, NA_PAIRS, HEADS * GRID_W, 2 * GRID_W), lambda b, i: (l, 0, 0, 0)),
        ],
        out_specs=pl.BlockSpec((tq, GW), lambda b, i: (b * nb + i, 0)),
        out_shape=jax.ShapeDtypeStruct((batch * seq, GW), F32),
        compiler_params=_params(("parallel", "arbitrary")),
        name="nbr_attn",
    )(pb, pb, pb, w["nbr_bias"])


def _nbr_bias(rpb):
    depth = rpb.shape[0]
    c = np.arange(GRID_W)
    dc = np.clip(c[None, :] - c[:, None], -(NA_COLS - 1), NA_COLS - 1) + NA_COLS - 1
    onehot = jnp.asarray(dc.reshape(-1)[None, :] == np.arange(2 * NA_COLS - 1)[:, None], F32)
    t = jnp.dot(rpb.reshape(-1, 2 * NA_COLS - 1), onehot, precision=lax.Precision.HIGHEST)
    t = t.reshape(depth, HEADS, 2 * NA_ROWS - 1, GRID_W, GRID_W)
    col_start = np.clip(c - NA_COLS // 2, 0, GRID_W - NA_COLS)
    col_ok = (c[None, :] >= col_start[:, None]) & (c[None, :] < col_start[:, None] + NA_COLS)
    t = jnp.where(jnp.asarray(col_ok), t, -jnp.inf)
    t = t.transpose(0, 2, 1, 3, 4).reshape(depth, 2 * NA_ROWS - 1, HEADS * GRID_W, GRID_W)
    return jnp.concatenate([t[:, :-1], t[:, 1:]], axis=-1)


TB_LRU = 512
SUB = 8


def _gelu_tanh(x):
    return 0.5 * x * (1.0 + jnp.tanh(0.7978845608028654 * (x + 0.044715 * x * x * x)))


def _lru_kernel(x_ref, xp_ref, xn_ref, cw_ref, cb_ref, wa_ref, ba_ref, wx_ref, bx_ref, lam_ref,
                o_ref, ext_ref, ae_ref, ue_ref, cin_ref, carry_ref, *, reverse):
    TB = TB_LRU
    i = pl.program_id(1)
    nb = pl.num_programs(1)
    seq_blk = (nb - 1 - i) if reverse else i

    @pl.when(i == 0)
    def _():
        carry_ref[...] = jnp.zeros_like(carry_ref)

    ext_ref[0:SUB, :] = jnp.where(seq_blk == 0, 0.0, xp_ref[...])
    ext_ref[SUB:SUB + TB, :] = x_ref[...]
    ext_ref[SUB + TB:, :] = jnp.where(seq_blk == nb - 1, 0.0, xn_ref[...])
    xc = cb_ref[...] + jnp.zeros((TB, GW), F32)
    for j in range(LRU_CONV):
        xc = xc + ext_ref[pl.ds(SUB - 2 + j, TB), :] * cw_ref[j:j + 1, :]

    xcb = xc.astype(BF16)
    r = _sigmoid(_dot(xcb, wa_ref[...]) + ba_ref[...])
    gi = _sigmoid(_dot(xcb, wx_ref[...]) + bx_ref[...])
    log_a = (-LRU_C) * r * _softplus(-lam_ref[...])
    a = jnp.exp(log_a)
    u = jnp.sqrt(-jnp.tanh(log_a) * (a * a + 1.0)) * (gi * xc)

    a = a.reshape(TB // SUB, SUB, GW)
    u = u.reshape(TB // SUB, SUB, GW)
    rm = lax.broadcasted_iota(jnp.int32, (TB // SUB, SUB, GW), 1)
    for d in (1, 2, 4):
        shift, ok = (SUB - d, rm < SUB - d) if reverse else (d, rm >= d)
        a_sh, u_sh = pltpu.roll(a, shift, 1), pltpu.roll(u, shift, 1)
        u = jnp.where(ok, a * u_sh + u, u)
        a = jnp.where(ok, a * a_sh, a)
    nt = TB // SUB
    edge = 0 if reverse else SUB - 1
    ae_ref[...] = jnp.broadcast_to(a[:, edge:edge + 1, :], a.shape).reshape(TB, GW)
    ue_ref[...] = jnp.broadcast_to(u[:, edge:edge + 1, :], u.shape).reshape(TB, GW)

    def body(t, carry):
        off = pl.multiple_of(((nt - 1 - t) if reverse else t) * SUB, SUB)
        cin_ref[pl.ds(off, SUB), :] = carry
        return ue_ref[pl.ds(off, SUB), :] + ae_ref[pl.ds(off, SUB), :] * carry

    carry_ref[...] = lax.fori_loop(0, nt, body, carry_ref[...], unroll=8)

    o_ref[...] = u.reshape(TB, GW) + a.reshape(TB, GW) * cin_ref[...]


def _lru_dir(pa, w, l, batch, seq, reverse):
    nb = seq // TB_LRU
    n = batch * seq
    tiles = TB_LRU // SUB
    e = 1 if reverse else 0

    def rb(b, i):
        return b * nb + ((nb - 1 - i) if reverse else i)

    per_layer = lambda rows: pl.BlockSpec((None, rows, GW), lambda b, i: (l, 0, 0))
    per_dir = lambda rows: pl.BlockSpec((None, None, rows, GW), lambda b, i: (l, e, 0, 0))
    in_specs = [
        pl.BlockSpec((TB_LRU, GW), lambda b, i: (rb(b, i), CB_XC)),
        pl.BlockSpec((SUB, GW), lambda b, i: (jnp.maximum(rb(b, i) * tiles - 1, 0), CB_XC)),
        pl.BlockSpec((SUB, GW), lambda b, i: (jnp.minimum((rb(b, i) + 1) * tiles, n // SUB - 1), CB_XC)),
        per_layer(LRU_CONV), per_layer(1), per_dir(GW), per_dir(1), per_dir(GW), per_dir(1), per_dir(1),
    ]
    args = [pa, pa, pa, w["lru_conv_w"], w["lru_conv_b"], w["lru_wa"], w["lru_ba"], w["lru_wx"], w["lru_bx"],
            w["lru_lam"]]
    return pl.pallas_call(
        functools.partial(_lru_kernel, reverse=reverse),
        grid=(batch, nb),
        in_specs=in_specs,
        out_specs=pl.BlockSpec((TB_LRU, GW), lambda b, i: (rb(b, i), 0)),
        out_shape=jax.ShapeDtypeStruct((n, GW), F32),
        scratch_shapes=[pltpu.VMEM((TB_LRU + 2 * SUB, GW), F32)] + [pltpu.VMEM((TB_LRU, GW), F32)] * 3
                       + [pltpu.VMEM((SUB, GW), F32)],
        compiler_params=_params(("arbitrary", "arbitrary")),
        name="lru_bwd" if reverse else "lru_fwd",
    )(*args)


SB_DIL = 128
GROUP_DIL = 2
WIN_DIL = SB_DIL + 2 * DIL_RADIUS
DIL_STEP = {1: (1024, 1), 4: (1024, 1), 16: (512, 2)}


def _band_masks():
    q = np.arange(HEADS * SB_DIL)[:, None] % SB_DIL
    c = np.arange(WIN_DIL)[None, :]
    offs = (0, -DIL_RADIUS, -2 * DIL_RADIUS)
    return jnp.asarray(np.stack([np.where(np.abs(c + off - q) <= DIL_RADIUS, 0.0, -np.inf) for off in offs]), F32)


def _dil_kernel(*refs, sub_len, tj, rps, merge):
    if merge:
        (q_ref, k_ref, v_ref, mask_ref, o4_ref, l4_ref, o16_ref, l16_ref, o_ref,
         n4o_ref, n4l_ref, n16o_ref, n16l_ref) = refs
        for d, src_o, src_l, dst_o, dst_l in ((4, o4_ref, l4_ref, n4o_ref, n4l_ref),
                                              (16, o16_ref, l16_ref, n16o_ref, n16l_ref)):
            for r in range(d):
                for t in range(GW // LANES):
                    lanes = slice(t * LANES, (t + 1) * LANES)
                    dst_o[t, pl.ds(r, tj // d, stride=d), :] = src_o[r, :, lanes]
                    dst_l[t, pl.ds(r, tj // d, stride=d), :] = src_l[r, :, lanes]
    else:
        q_ref, k_ref, v_ref, mask_ref, o_ref, l_ref = refs
    j_blk = pl.program_id(2) * tj
    blocks = [(rr, sb) for rr in range(rps) for sb in range(tj // SB_DIL)]
    for g in range(0, len(blocks), GROUP_DIL):
        group = blocks[g:g + GROUP_DIL]
        scores, values = [], []
        for rr, sb in group:
            rows = slice(sb * SB_DIL, (sb + 1) * SB_DIL)
            j0 = j_blk + sb * SB_DIL
            ws = pl.multiple_of(jnp.clip(j0 - DIL_RADIUS, 0, sub_len - WIN_DIL), DIL_RADIUS)
            which = 1 - (j0 == 0).astype(jnp.int32) + (j0 == sub_len - SB_DIL).astype(jnp.int32)
            s = _dot_nt(_stack_heads(q_ref[rr, rows, :]), k_ref[rr, pl.ds(ws, WIN_DIL), :])
            scores.append(s + mask_ref[which])
            values.append(v_ref[rr, pl.ds(ws, WIN_DIL), :])
        for (rr, sb), (o, lse) in zip(group, _softmax_pv(scores, values)):
            rows = slice(sb * SB_DIL, (sb + 1) * SB_DIL)
            if merge:
                tok = lambda ref: jnp.concatenate([ref[t, rows, :] for t in range(GW // LANES)], axis=1)
                l4, l16 = tok(n4l_ref), tok(n16l_ref)
                m = jnp.maximum(jnp.maximum(lse, l4), l16)
                e1, e4, e16 = jnp.exp(lse - m), jnp.exp(l4 - m), jnp.exp(l16 - m)
                o_ref[rows, :] = (e1 * o + e4 * tok(n4o_ref) + e16 * tok(n16o_ref)) / (e1 + e4 + e16)
            else:
                o_ref[rr, rows, :] = o
                l_ref[rr, rows, :] = lse


def _dil_branch(qkv, masks, batch, seq, dil, others=None):
    sub_len = seq // dil
    tj, rps = DIL_STEP[dil]
    nj = sub_len // tj
    col = lambda cb, rows, jmap: pl.BlockSpec((None, rps, rows, GW), lambda b, r, j: (b, r, jmap(j), cb))
    in_specs = [col(0, tj, lambda j: j), col(1, sub_len, lambda j: 0), col(2, sub_len, lambda j: 0),
                pl.BlockSpec(masks.shape, lambda b, r, j: (0, 0, 0))]
    out_blk = pl.BlockSpec((None, rps, tj, GW), lambda b, r, j: (b, r, j, 0))
    out_sd = jax.ShapeDtypeStruct((batch, dil, sub_len, GW), F32)
    args = [qkv, qkv, qkv, masks]
    scratch = []
    if others is None:
        out_specs, out_shape = [out_blk, out_blk], [out_sd, out_sd]
    else:
        assert dil == 1 and rps == 1
        for d in (4, 16):
            in_specs += [pl.BlockSpec((None, d, tj // d, GW), lambda b, r, j: (b, 0, j, 0))] * 2
        args += list(others)
        out_specs = pl.BlockSpec((tj, GW), lambda b, r, j: (b * nj + j, 0))
        out_shape = jax.ShapeDtypeStruct((batch * seq, GW), F32)
        scratch = [pltpu.VMEM((GW // LANES, tj, LANES), F32)] * 4
    return pl.pallas_call(
        functools.partial(_dil_kernel, sub_len=sub_len, tj=tj, rps=rps, merge=others is not None),
        grid=(batch, dil // rps, nj),
        in_specs=in_specs,
        out_specs=out_specs,
        out_shape=out_shape,
        scratch_shapes=scratch,
        compiler_params=_params(("parallel", "parallel", "arbitrary")),
        name=f"dil_attn_d{dil}",
    )(*args)


def _reorder_w_in_kernel(w_ref, o_ref):
    za = 4 * GW
    qb = za + 2 * GLA_RANK
    xc = qb + QKV
    qd = xc + 2 * GW
    o_ref[0:za, :] = w_ref[0:za, :].astype(BF16)
    o_ref[za:Z_COL0, :] = w_ref[xc:qd, :].astype(BF16)
    o_ref[Z_COL0:Z_COL0 + 2 * GLA_RANK, :] = w_ref[za:qb, :].astype(BF16)
    o_ref[Z_COL0 + 2 * GLA_RANK:PA_COLS, :] = jnp.zeros((LANES - 2 * GLA_RANK, D_MODEL), BF16)
    o_ref[PA_COLS:PA_COLS + QKV, :] = w_ref[qb:xc, :].astype(BF16)
    o_ref[PA_COLS + QKV:P_COLS, :] = w_ref[qd:qd + QKV, :].astype(BF16)


def _reorder_w_in(w_in):
    depth, rows, cols = w_in.shape
    return pl.pallas_call(
        _reorder_w_in_kernel,
        grid=(depth,),
        in_specs=[pl.BlockSpec((None, cols, rows), lambda l: (l, 0, 0))],
        out_specs=pl.BlockSpec((None, P_COLS, rows), lambda l: (l, 0, 0)),
        out_shape=jax.ShapeDtypeStruct((depth, P_COLS, rows), BF16),
        compiler_params=_params(("parallel",)),
        name="w_in_layout",
    )(jnp.swapaxes(w_in, 1, 2))


def _block_diag(w):
    eye = jnp.eye(HEADS, dtype=w.dtype)[:, None, :, None]
    out = w[..., :, :, None, :] * eye
    return out.reshape(w.shape[:-3] + (GW, GW))


def _gate_weights(w_gate):
    per_dir = [jnp.pad(w_gate[:, e], ((0, 0), (e * GLA_RANK, LANES - (e + 1) * GLA_RANK), (0, 0))) for e in (0, 1)]
    return jnp.stack(per_dir, axis=1).astype(BF16)


def _rope_tables(seq):
    pos = jnp.arange(seq, dtype=F32)
    inv_freq = ROPE_THETA ** (-jnp.arange(0, HEAD_DIM, 2, dtype=F32) / HEAD_DIM)
    ang = pos[:, None] * inv_freq[None, :]
    cos, sin = jnp.cos(ang), jnp.sin(ang)
    return jnp.tile(jnp.concatenate([cos, cos], -1), (1, HEADS)), jnp.tile(jnp.concatenate([-sin, sin], -1), (1, HEADS))


def kernel(x, mix_norm_pre, mix_norm_post, w_in, gla_w_gate, gla_b_gate, gla_norm, na_rpb, lru_conv_w, lru_conv_b, lru_w_a, lru_b_a, lru_w_x, lru_b_x, lru_lambda, w_out, ffn_norm_pre, ffn_norm_post, ffn_w_in, ffn_w_out):
    batch, seq, d_model = x.shape
    assert d_model == D_MODEL and seq % TB_GLA == 0 and seq % (NA_BLOCK_ROWS * GRID_W) == 0
    assert all(seq % (d * DIL_STEP[d][0]) == 0 and d % DIL_STEP[d][1] == 0 for d in DILATIONS)
    depth = w_in.shape[0]
    n = batch * seq
    xf = x.reshape(n, D_MODEL)
    cos_t, sin_t = _rope_tables(seq)
    masks = _band_masks()
    rows = lambda t: t[..., None, :]
    w = dict(
        mix_pre=rows(mix_norm_pre), mix_post=rows(mix_norm_post), w_in=_reorder_w_in(w_in),
        gla_wg=_gate_weights(gla_w_gate), gla_bg=rows(gla_b_gate), gla_norm=rows(gla_norm),
        nbr_bias=_nbr_bias(na_rpb),
        lru_conv_w=lru_conv_w, lru_conv_b=rows(lru_conv_b),
        lru_wa=_block_diag(lru_w_a).astype(BF16), lru_ba=rows(lru_b_a),
        lru_wx=_block_diag(lru_w_x).astype(BF16), lru_bx=rows(lru_b_x), lru_lam=rows(lru_lambda),
        w_out=w_out.astype(BF16), ffn_pre=rows(ffn_norm_pre), ffn_post=rows(ffn_norm_post),
        ffn_in=ffn_w_in.astype(BF16), ffn_out=ffn_w_out.astype(BF16),
    )

    for l in range(depth):
        pa, pb, d1, d4, d16 = _in_proj(xf, w, l, cos_t, sin_t, batch, seq)

        gla_dirs = _gla(pa, w, l, batch, seq)
        yb = _nbr(pb, w, l, batch, seq)
        lru_dirs = (_lru_dir(pa, w, l, batch, seq, False), _lru_dir(pa, w, l, batch, seq, True))
        o4, l4 = _dil_branch(d4, masks, batch, seq, 4)
        o16, l16 = _dil_branch(d16, masks, batch, seq, 16)
        yd = _dil_branch(d1, masks, batch, seq, 1, others=(o4, l4, o16, l16))

        xf = _tail(pa, gla_dirs, yb, lru_dirs, yd, xf, w, l)
    return xf.reshape(batch, seq, D_MODEL)
```

```python
import functools

import numpy as np
import jax
import jax.numpy as jnp
from jax import lax
from jax.experimental import pallas as pl
from jax.experimental.pallas import tpu as pltpu

F32 = jnp.float32
BF16 = jnp.bfloat16

D_MODEL = 1024
HEAD_DIM = 64
HEADS = 4
GW = HEADS * HEAD_DIM
GLA_RANK = 16
GLA_TAU = 16.0
GLA_CHUNK = 64
GRID_W = 64
NA_ROWS = 8
NA_COLS = 16
LRU_C = 8.0
LRU_CONV = 4
DILATIONS = (1, 4, 16)
DIL_RADIUS = 64
ROPE_THETA = 10000.0
D_FF = 2816
EPS = 1e-6
QK_SCALE = HEAD_DIM ** -0.5
LANES = 128

CB_QA, CB_KA, CB_VA, CB_GA, CB_XC, CB_GC = 0, 1, 2, 3, 4, 5
Z_COL0 = 6 * GW
PA_COLS = Z_COL0 + LANES
QKV = 3 * GW
P_COLS = PA_COLS + 2 * QKV

VMEM_LIMIT = 56 * 1024 * 1024


def _params(sem, vmem=VMEM_LIMIT):
    return pltpu.CompilerParams(dimension_semantics=sem, vmem_limit_bytes=vmem)


def _stack_heads(t):
    rows = t.shape[0]
    lane = lax.broadcasted_iota(jnp.int32, (rows, LANES), 1)
    zero = jnp.zeros((rows, LANES), t.dtype)
    blocks = []
    for h in range(HEADS):
        half = t[:, (h // 2) * LANES:(h // 2 + 1) * LANES]
        half = jnp.where((lane < HEAD_DIM) if h % 2 == 0 else (lane >= HEAD_DIM), half, zero)
        blocks.append(jnp.concatenate([half, zero] if h < 2 else [zero, half], axis=1))
    return jnp.concatenate(blocks, axis=0)


def _unstack_heads(t):
    rows = t.shape[0] // HEADS
    first = lax.broadcasted_iota(jnp.int32, (rows, LANES), 1) < HEAD_DIM
    blk = lambda h, tile: t[h * rows:(h + 1) * rows, tile * LANES:(tile + 1) * LANES]
    return jnp.concatenate([jnp.where(first, blk(0, 0), blk(1, 0)), jnp.where(first, blk(2, 1), blk(3, 1))], axis=1)


def _dot(a, b):
    return jnp.dot(a, b, preferred_element_type=F32)


def _dot_nt(a, b):
    return lax.dot_general(a, b, (((1,), (1,)), ((), ())), preferred_element_type=F32)


def _dot_tn(a, b):
    return lax.dot_general(a, b, (((0,), (0,)), ((), ())), preferred_element_type=F32)


def _split(t):
    hi = t.astype(BF16)
    return hi, (t - hi.astype(F32)).astype(BF16)


def _split_dot(m, t):
    hi, lo = _split(t)
    return _dot(m, hi) + _dot(m, lo)


def _split_dot_rhs(t, m):
    hi, lo = _split(t)
    return _dot(hi, m) + _dot(lo, m)


def _rms(x, g):
    ms = jnp.mean(x * x, axis=-1, keepdims=True)
    return x * lax.rsqrt(ms + EPS) * g


def _sigmoid(x):
    return 1.0 / (1.0 + jnp.exp(-x))


def _softplus(x):
    return jnp.maximum(x, 0.0) + jnp.log1p(jnp.exp(-jnp.abs(x)))


def _softmax_pv(scores, values):
    maxes = [jnp.max(s, axis=-1, keepdims=True) for s in scores]
    exps = [jnp.exp(s - m) for s, m in zip(scores, maxes)]
    sums = [jnp.sum(e, axis=-1, keepdims=True) for e in exps]
    pvs = [_dot(e.astype(BF16), v) for e, v in zip(exps, values)]
    out = []
    for pv, m, d in zip(pvs, maxes, sums):
        den = _unstack_heads(jnp.broadcast_to(d, pv.shape))
        o = _unstack_heads(pv) / den
        lse = _unstack_heads(jnp.broadcast_to(m, pv.shape)) + jnp.log(den)
        out.append((o, lse))
    return out


TM_IN = 512


def _rope(t, c, s):
    lane = lax.broadcasted_iota(jnp.int32, t.shape, 1)
    first_half = (lane & (HEAD_DIM - 1)) < HEAD_DIM // 2
    swapped = jnp.where(first_half, pltpu.roll(t, GW - HEAD_DIM // 2, 1), pltpu.roll(t, HEAD_DIM // 2, 1))
    return t * c + swapped * s


def _in_proj_kernel(x_ref, g_ref, w_ref, c_ref, s_ref, pa_ref, pb_ref, d1_ref, d4_ref, d16_ref, rope_ref):
    h = _rms(x_ref[...], g_ref[...]).astype(BF16)
    pd = _dot_nt(h, w_ref[PA_COLS + QKV:P_COLS, :])
    pb = _dot_nt(h, w_ref[PA_COLS:PA_COLS + QKV, :])
    pb_ref[:, 0:GW] = (pb[:, 0:GW] * QK_SCALE).astype(BF16)
    pb_ref[:, GW:QKV] = pb[:, GW:QKV].astype(BF16)
    c = c_ref[...]
    s = s_ref[...]
    pa_ref[...] = _dot_nt(h, w_ref[0:PA_COLS, :])
    qkv = jnp.concatenate([_rope(pd[:, 0:GW], c, s) * QK_SCALE, _rope(pd[:, GW:2 * GW], c, s),
                           pd[:, 2 * GW:QKV]], axis=1)
    d1_ref[0] = qkv.astype(BF16)
    for t in range(QKV // LANES):
        lanes = slice(t * LANES, (t + 1) * LANES)
        rope_ref[t] = qkv[:, lanes]
        for d, ref in ((4, d4_ref), (16, d16_ref)):
            for r in range(d):
                ref[r, :, lanes] = rope_ref[t, pl.ds(r, TM_IN // d, stride=d), :].astype(BF16)


def _in_proj(x, w, l, cos_t, sin_t, batch, seq):
    n = x.shape[0]
    nl = seq // TM_IN
    row = lambda cols: pl.BlockSpec((TM_IN, cols), lambda i: (i, 0))
    tab = pl.BlockSpec((TM_IN, GW), lambda i: (i % nl, 0))
    res = lambda d: pl.BlockSpec((None, d, TM_IN // d, QKV), lambda i: (i // nl, 0, i % nl, 0))
    return pl.pallas_call(
        _in_proj_kernel,
        grid=(n // TM_IN,),
        in_specs=[row(D_MODEL),
                  pl.BlockSpec((None, 1, D_MODEL), lambda i: (l, 0, 0)),
                  pl.BlockSpec((None, P_COLS, D_MODEL), lambda i: (l, 0, 0)),
                  tab, tab],
        out_specs=[row(PA_COLS), row(QKV), res(1), res(4), res(16)],
        out_shape=[jax.ShapeDtypeStruct((n, PA_COLS), F32),
                   jax.ShapeDtypeStruct((n, QKV), BF16),
                   jax.ShapeDtypeStruct((batch, 1, seq, QKV), BF16),
                   jax.ShapeDtypeStruct((batch, 4, seq // 4, QKV), BF16),
                   jax.ShapeDtypeStruct((batch, 16, seq // 16, QKV), BF16)],
        scratch_shapes=[pltpu.VMEM((QKV // LANES, TM_IN, LANES), F32)],
        compiler_params=_params(("parallel",)),
        name="in_proj",
    )(x, w["mix_pre"], w["w_in"], cos_t, sin_t)


TM_FFN = 512
MXU_TILE = 256
FF_CHUNKS = ((0, 6 * MXU_TILE), (6 * MXU_TILE, D_FF))


def _tail_kernel(af_ref, ab_ref, ag_ref, an_ref, yb_ref, cf_ref, cb_ref, cg_ref, yd_ref,
                 wm_ref, x_ref, gm_ref, g1_ref, wi_ref, wo_ref, g2_ref, o_ref):
    ya = _gla_finalize(af_ref[...] + ab_ref[...], ag_ref[...], an_ref[...])
    yc = (cf_ref[...] + cb_ref[...]) * _gelu_tanh(cg_ref[...])
    y = _dot(ya.astype(BF16), wm_ref[0 * GW:1 * GW, :])
    y += _dot(yb_ref[...].astype(BF16), wm_ref[1 * GW:2 * GW, :])
    y += _dot(yc.astype(BF16), wm_ref[2 * GW:3 * GW, :])
    y += _dot(yd_ref[...].astype(BF16), wm_ref[3 * GW:4 * GW, :])
    x1 = x_ref[...] + _rms(y, gm_ref[...])
    h = _rms(x1, g1_ref[...]).astype(BF16)
    f = None
    for lo, hi in FF_CHUNKS:
        gate = _dot(h, wi_ref[:, lo:hi])
        up = _dot(h, wi_ref[:, D_FF + lo:D_FF + hi])
        part = _dot((gate * _sigmoid(gate) * up).astype(BF16), wo_ref[lo:hi, :])
        f = part if f is None else f + part
    o_ref[...] = x1 + _rms(f, g2_ref[...])


def _tail(pa, gla_dirs, yb, lru_dirs, yd, x, w, l):
    n = x.shape[0]
    row = pl.BlockSpec((TM_FFN, GW), lambda i: (i, 0))
    stream = lambda cb: pl.BlockSpec((TM_FFN, GW), lambda i: (i, cb))
    gain = pl.BlockSpec((None, 1, D_MODEL), lambda i: (l, 0, 0))
    resident = lambda r, c: pl.BlockSpec((None, r, c), lambda i: (l, 0, 0), pipeline_mode=pl.Buffered(1))
    return pl.pallas_call(
        _tail_kernel,
        grid=(n // TM_FFN,),
        in_specs=[row, row, stream(CB_GA), pl.BlockSpec((None, 1, GW), lambda i: (l, 0, 0)), row,
                  row, row, stream(CB_GC), row] + [
            resident(D_MODEL, D_MODEL),
            pl.BlockSpec((TM_FFN, D_MODEL), lambda i: (i, 0)),
            gain, gain,
            resident(D_MODEL, 2 * D_FF),
            resident(D_FF, D_MODEL),
            gain,
        ],
        out_specs=pl.BlockSpec((TM_FFN, D_MODEL), lambda i: (i, 0)),
        out_shape=jax.ShapeDtypeStruct((n, D_MODEL), F32),
        compiler_params=_params(("parallel",)),
        name="out_proj_ffn",
    )(gla_dirs[0], gla_dirs[1], pa, w["gla_norm"], yb, lru_dirs[0], lru_dirs[1], pa, yd,
      w["w_out"], x, w["mix_post"], w["ffn_pre"], w["ffn_in"], w["ffn_out"], w["ffn_post"])


TB_GLA = 1024


def _same_head_tile():
    rb = lax.broadcasted_iota(jnp.int32, (LANES, LANES), 0) >> 6
    cb = lax.broadcasted_iota(jnp.int32, (LANES, LANES), 1) >> 6
    return rb == cb


def _gla_direction(q_ref, k_ref, v_ref, z_ref, wg_ref, bg_ref, o_ref, st_ref, reverse):
    C = GLA_CHUNK
    n_chunks = TB_GLA // C
    logit = _dot(z_ref[...].astype(BF16), wg_ref[...]) + bg_ref[...]
    log_a = -_softplus(-logit) * (1.0 / GLA_TAU)

    r_i = lax.broadcasted_iota(jnp.int32, (C, C), 0)
    c_i = lax.broadcasted_iota(jnp.int32, (C, C), 1)
    cum = jnp.where((c_i >= r_i) if reverse else (c_i <= r_i), 1.0, 0.0).astype(BF16)
    t_i = lax.broadcasted_iota(jnp.int32, (C, LANES), 0)
    s_i = lax.broadcasted_iota(jnp.int32, (C, LANES), 1) & (C - 1)
    causal = (s_i >= t_i) if reverse else (s_i <= t_i)
    same_head = _same_head_tile()
    last, mid = (0, C // 2) if reverse else (C - 1, C // 2 - 1)

    def stack2(t):
        lane = lax.broadcasted_iota(jnp.int32, t.shape, 1)
        zero = jnp.zeros_like(t)
        return jnp.concatenate([jnp.where(lane < HEAD_DIM, t, zero), jnp.where(lane >= HEAD_DIM, t, zero)], axis=0)

    b_c = [_split_dot(cum, log_a[c * C:(c + 1) * C]) for c in range(n_chunks)]
    b = jnp.concatenate(b_c, axis=0)
    b_mid = jnp.concatenate([jnp.broadcast_to(t[mid:mid + 1], (C, GW)) for t in b_c], axis=0)
    q = q_ref[...] * QK_SCALE
    k = k_ref[...]
    v = v_ref[...].astype(BF16)
    q_mid = q * jnp.exp(b - b_mid)
    k_mid = k * jnp.exp(b_mid - b)
    q_in = q_mid.astype(BF16)
    k_in = k_mid.astype(BF16)
    q_st = jnp.concatenate([q_mid[c * C:(c + 1) * C] * jnp.exp(t[mid:mid + 1]) for c, t in enumerate(b_c)],
                           axis=0).astype(BF16)
    k_st = jnp.concatenate([k_mid[c * C:(c + 1) * C] * jnp.exp(t[last:last + 1] - t[mid:mid + 1])
                            for c, t in enumerate(b_c)], axis=0).astype(BF16)

    yield

    chunk = [slice(c * C, (c + 1) * C) for c in range(n_chunks)]
    order = range(n_chunks - 1, -1, -1) if reverse else range(n_chunks)
    for tile in range(GW // LANES):
        lanes = slice(tile * LANES, (tile + 1) * LANES)
        att = [_dot_nt(q_in[r, lanes], stack2(k_in[r, lanes])) for r in chunk]
        yield
        o_in = [_dot(jnp.where(causal, a, 0.0).astype(BF16), stack2(v[r, lanes])) for a, r in zip(att, chunk)]
        upd = [_dot_tn(v[r, lanes], k_st[r, lanes]) for r in chunk]
        yield
        st = st_ref[tile]
        for c in order:
            o_ref[chunk[c], lanes] = o_in[c] + _dot_nt(q_st[chunk[c], lanes], st.astype(BF16))
            st = st * jnp.exp(b_c[c][last:last + 1, lanes]) + jnp.where(same_head, upd[c], 0.0)
            yield
        st_ref[tile] = st


def _gla_kernel(qf_ref, kf_ref, vf_ref, zf_ref, qb_ref, kb_ref, vb_ref, zb_ref, wg_ref, bg_ref,
                of_ref, ob_ref, stf_ref, stb_ref):
    @pl.when(pl.program_id(1) == 0)
    def _():
        stf_ref[...] = jnp.zeros_like(stf_ref)
        stb_ref[...] = jnp.zeros_like(stb_ref)

    stages = [_gla_direction(qf_ref, kf_ref, vf_ref, zf_ref, wg_ref.at[0], bg_ref.at[0], of_ref, stf_ref, False),
              _gla_direction(qb_ref, kb_ref, vb_ref, zb_ref, wg_ref.at[1], bg_ref.at[1], ob_ref, stb_ref, True)]
    while stages:
        stages = [s for s in stages if next(s, True) is None]


def _gla_finalize(o, g, norm_g):
    blk = jnp.where(_same_head_tile(), 1.0 / HEAD_DIM, 0.0).astype(BF16)
    sq = o * o
    ms = jnp.concatenate([_split_dot_rhs(sq[:, t * LANES:(t + 1) * LANES], blk) for t in range(GW // LANES)],
                         axis=1)
    return o * lax.rsqrt(ms + EPS) * norm_g * (g * _sigmoid(g))


def _gla(pa, w, l, batch, seq):
    nb = seq // TB_GLA
    n = batch * seq
    fwd = lambda b, i: b * nb + i
    bwd = lambda b, i: b * nb + nb - 1 - i
    col = lambda rb, cb: pl.BlockSpec((TB_GLA, GW), lambda b, i: (rb(b, i), cb))
    gate = lambda rb: pl.BlockSpec((TB_GLA, LANES), lambda b, i: (rb(b, i), Z_COL0 // LANES))
    out_sd = jax.ShapeDtypeStruct((n, GW), F32)
    state = pltpu.VMEM((GW // LANES, LANES, LANES), F32)
    return pl.pallas_call(
        _gla_kernel,
        grid=(batch, nb),
        in_specs=[col(fwd, CB_QA), col(fwd, CB_KA), col(fwd, CB_VA), gate(fwd),
                  col(bwd, CB_QA), col(bwd, CB_KA), col(bwd, CB_VA), gate(bwd),
                  pl.BlockSpec((None, 2, LANES, GW), lambda b, i: (l, 0, 0, 0)),
                  pl.BlockSpec((None, 2, 1, GW), lambda b, i: (l, 0, 0, 0))],
        out_specs=[col(fwd, 0), col(bwd, 0)],
        out_shape=[out_sd, out_sd],
        scratch_shapes=[state, state],
        compiler_params=_params(("arbitrary", "arbitrary")),
        name="gla",
    )(pa, pa, pa, pa, pa, pa, pa, pa, w["gla_wg"], w["gla_bg"])


NA_BLOCK_ROWS = 16
NA_GROUP = 16
NA_PAIRS = 2 * NA_ROWS - 2


def _nbr_kernel(q_ref, k_ref, v_ref, bias_ref, o_ref, *, grid_rows):
    W = GRID_W
    win = NA_ROWS * W
    row0 = pl.program_id(1) * NA_BLOCK_ROWS

    def body(g, carry):
        scores, starts = [], []
        for j in range(NA_GROUP):
            r = row0 + g * NA_GROUP + j
            start = jnp.clip(r - NA_ROWS // 2, 0, grid_rows - NA_ROWS)
            pat = start - r + (NA_ROWS - 1)
            k0 = pl.multiple_of(start * W, W)
            q0 = pl.multiple_of((g * NA_GROUP + j) * W, W)
            s = _dot_nt(_stack_heads(q_ref[pl.ds(q0, W), :]), k_ref[pl.ds(k0, win), :])
            scores.append(s + jnp.concatenate([bias_ref[pat + 2 * t] for t in range(NA_ROWS // 2)], axis=1))
            starts.append(k0)
        results = _softmax_pv(scores, [v_ref[pl.ds(k0, win), :] for k0 in starts])
        for j, (o, _) in enumerate(results):
            o_ref[pl.ds(pl.multiple_of((g * NA_GROUP + j) * W, W), W), :] = o
        return carry

    lax.fori_loop(0, NA_BLOCK_ROWS // NA_GROUP, body, 0)


def _nbr(pb, w, l, batch, seq):
    grid_rows = seq // GRID_W
    tq = NA_BLOCK_ROWS * GRID_W
    nb = seq // tq
    return pl.pallas_call(
        functools.partial(_nbr_kernel, grid_rows=grid_rows),
        grid=(batch, nb),
        in_specs=[
            pl.BlockSpec((tq, GW), lambda b, i: (b * nb + i, 0)),
            pl.BlockSpec((seq, GW), lambda b, i: (b, 1)),
            pl.BlockSpec((seq, GW), lambda b, i: (b, 2)),
            pl.BlockSpec((None, NA_PAIRS, HEADS * GRID_W, 2 * GRID_W), lambda b, i: (l, 0, 0, 0)),
        ],
        out_specs=pl.BlockSpec((tq, GW), lambda b, i: (b * nb + i, 0)),
        out_shape=jax.ShapeDtypeStruct((batch * seq, GW), F32),
        compiler_params=_params(("parallel", "arbitrary")),
        name="nbr_attn",
    )(pb, pb, pb, w["nbr_bias"])


def _nbr_bias(rpb):
    depth = rpb.shape[0]
    c = np.arange(GRID_W)
    dc = np.clip(c[None, :] - c[:, None], -(NA_COLS - 1), NA_COLS - 1) + NA_COLS - 1
    onehot = jnp.asarray(dc.reshape(-1)[None, :] == np.arange(2 * NA_COLS - 1)[:, None], F32)
    t = jnp.dot(rpb.reshape(-1, 2 * NA_COLS - 1), onehot, precision=lax.Precision.HIGHEST)
    t = t.reshape(depth, HEADS, 2 * NA_ROWS - 1, GRID_W, GRID_W)
    col_start = np.clip(c - NA_COLS // 2, 0, GRID_W - NA_COLS)
    col_ok = (c[None, :] >= col_start[:, None]) & (c[None, :] < col_start[:, None] + NA_COLS)
    t = jnp.where(jnp.asarray(col_ok), t, -jnp.inf)
    t = t.transpose(0, 2, 1, 3, 4).reshape(depth, 2 * NA_ROWS - 1, HEADS * GRID_W, GRID_W)
    return jnp.concatenate([t[:, :-1], t[:, 1:]], axis=-1)


TB_LRU = 1024
SUB = 8


def _gelu_tanh(x):
    return 0.5 * x * (1.0 + jnp.tanh(0.7978845608028654 * (x + 0.044715 * x * x * x)))


def _lru_kernel(x_ref, xp_ref, xn_ref, cw_ref, cb_ref, wa_ref, ba_ref, wx_ref, bx_ref, lam_ref,
                o_ref, ext_ref, ae_ref, ue_ref, cin_ref, carry_ref, *, reverse):
    TB = TB_LRU
    i = pl.program_id(1)
    nb = pl.num_programs(1)
    seq_blk = (nb - 1 - i) if reverse else i

    @pl.when(i == 0)
    def _():
        carry_ref[...] = jnp.zeros_like(carry_ref)

    ext_ref[0:SUB, :] = jnp.where(seq_blk == 0, 0.0, xp_ref[...])
    ext_ref[SUB:SUB + TB, :] = x_ref[...]
    ext_ref[SUB + TB:, :] = jnp.where(seq_blk == nb - 1, 0.0, xn_ref[...])
    xc = cb_ref[...] + jnp.zeros((TB, GW), F32)
    for j in range(LRU_CONV):
        xc = xc + ext_ref[pl.ds(SUB - 2 + j, TB), :] * cw_ref[j:j + 1, :]

    xcb = xc.astype(BF16)
    r = _sigmoid(_dot(xcb, wa_ref[...]) + ba_ref[...])
    gi = _sigmoid(_dot(xcb, wx_ref[...]) + bx_ref[...])
    log_a = (-LRU_C) * r * _softplus(-lam_ref[...])
    a = jnp.exp(log_a)
    u = jnp.sqrt(-jnp.tanh(log_a) * (a * a + 1.0)) * (gi * xc)

    a = a.reshape(TB // SUB, SUB, GW)
    u = u.reshape(TB // SUB, SUB, GW)
    rm = lax.broadcasted_iota(jnp.int32, (TB // SUB, SUB, GW), 1)
    for d in (1, 2, 4):
        shift, ok = (SUB - d, rm < SUB - d) if reverse else (d, rm >= d)
        a_sh, u_sh = pltpu.roll(a, shift, 1), pltpu.roll(u, shift, 1)
        u = jnp.where(ok, a * u_sh + u, u)
        a = jnp.where(ok, a * a_sh, a)
    nt = TB // SUB
    edge = 0 if reverse else SUB - 1
    ae_ref[...] = jnp.broadcast_to(a[:, edge:edge + 1, :], a.shape).reshape(TB, GW)
    ue_ref[...] = jnp.broadcast_to(u[:, edge:edge + 1, :], u.shape).reshape(TB, GW)

    def body(t, carry):
        off = pl.multiple_of(((nt - 1 - t) if reverse else t) * SUB, SUB)
        cin_ref[pl.ds(off, SUB), :] = carry
        return ue_ref[pl.ds(off, SUB), :] + ae_ref[pl.ds(off, SUB), :] * carry

    carry_ref[...] = lax.fori_loop(0, nt, body, carry_ref[...], unroll=8)

    o_ref[...] = u.reshape(TB, GW) + a.reshape(TB, GW) * cin_ref[...]


def _lru_dir(pa, w, l, batch, seq, reverse):
    nb = seq // TB_LRU
    n = batch * seq
    tiles = TB_LRU // SUB
    e = 1 if reverse else 0

    def rb(b, i):
        return b * nb + ((nb - 1 - i) if reverse else i)

    per_layer = lambda rows: pl.BlockSpec((None, rows, GW), lambda b, i: (l, 0, 0))
    per_dir = lambda rows: pl.BlockSpec((None, None, rows, GW), lambda b, i: (l, e, 0, 0))
    in_specs = [
        pl.BlockSpec((TB_LRU, GW), lambda b, i: (rb(b, i), CB_XC)),
        pl.BlockSpec((SUB, GW), lambda b, i: (jnp.maximum(rb(b, i) * tiles - 1, 0), CB_XC)),
        pl.BlockSpec((SUB, GW), lambda b, i: (jnp.minimum((rb(b, i) + 1) * tiles, n // SUB - 1), CB_XC)),
        per_layer(LRU_CONV), per_layer(1), per_dir(GW), per_dir(1), per_dir(GW), per_dir(1), per_dir(1),
    ]
    args = [pa, pa, pa, w["lru_conv_w"], w["lru_conv_b"], w["lru_wa"], w["lru_ba"], w["lru_wx"], w["lru_bx"],
            w["lru_lam"]]
    return pl.pallas_call(
        functools.partial(_lru_kernel, reverse=reverse),
        grid=(batch, nb),
        in_specs=in_specs,
        out_specs=pl.BlockSpec((TB_LRU, GW), lambda b, i: (rb(b, i), 0)),
        out_shape=jax.ShapeDtypeStruct((n, GW), F32),
        scratch_shapes=[pltpu.VMEM((TB_LRU + 2 * SUB, GW), F32)] + [pltpu.VMEM((TB_LRU, GW), F32)] * 3
                       + [pltpu.VMEM((SUB, GW), F32)],
        compiler_params=_params(("arbitrary", "arbitrary")),
        name="lru_bwd" if reverse else "lru_fwd",
    )(*args)


SB_DIL = 128
GROUP_DIL = 2
WIN_DIL = SB_DIL + 2 * DIL_RADIUS
DIL_STEP = {1: (1024, 1), 4: (1024, 1), 16: (512, 2)}


def _band_masks():
    q = np.arange(HEADS * SB_DIL)[:, None] % SB_DIL
    c = np.arange(WIN_DIL)[None, :]
    offs = (0, -DIL_RADIUS, -2 * DIL_RADIUS)
    return jnp.asarray(np.stack([np.where(np.abs(c + off - q) <= DIL_RADIUS, 0.0, -np.inf) for off in offs]), F32)


def _dil_kernel(*refs, sub_len, tj, rps, merge):
    if merge:
        (q_ref, k_ref, v_ref, mask_ref, o4_ref, l4_ref, o16_ref, l16_ref, o_ref,
         n4o_ref, n4l_ref, n16o_ref, n16l_ref) = refs
        for d, src_o, src_l, dst_o, dst_l in ((4, o4_ref, l4_ref, n4o_ref, n4l_ref),
                                              (16, o16_ref, l16_ref, n16o_ref, n16l_ref)):
            for r in range(d):
                for t in range(GW // LANES):
                    lanes = slice(t * LANES, (t + 1) * LANES)
                    dst_o[t, pl.ds(r, tj // d, stride=d), :] = src_o[r, :, lanes]
                    dst_l[t, pl.ds(r, tj // d, stride=d), :] = src_l[r, :, lanes]
    else:
        q_ref, k_ref, v_ref, mask_ref, o_ref, l_ref = refs
    j_blk = pl.program_id(2) * tj
    blocks = [(rr, sb) for rr in range(rps) for sb in range(tj // SB_DIL)]
    for g in range(0, len(blocks), GROUP_DIL):
        group = blocks[g:g + GROUP_DIL]
        scores, values = [], []
        for rr, sb in group:
            rows = slice(sb * SB_DIL, (sb + 1) * SB_DIL)
            j0 = j_blk + sb * SB_DIL
            ws = pl.multiple_of(jnp.clip(j0 - DIL_RADIUS, 0, sub_len - WIN_DIL), DIL_RADIUS)
            which = 1 - (j0 == 0).astype(jnp.int32) + (j0 == sub_len - SB_DIL).astype(jnp.int32)
            s = _dot_nt(_stack_heads(q_ref[rr, rows, :]), k_ref[rr, pl.ds(ws, WIN_DIL), :])
            scores.append(s + mask_ref[which])
            values.append(v_ref[rr, pl.ds(ws, WIN_DIL), :])
        for (rr, sb), (o, lse) in zip(group, _softmax_pv(scores, values)):
            rows = slice(sb * SB_DIL, (sb + 1) * SB_DIL)
            if merge:
                tok = lambda ref: jnp.concatenate([ref[t, rows, :] for t in range(GW // LANES)], axis=1)
                l4, l16 = tok(n4l_ref), tok(n16l_ref)
                m = jnp.maximum(jnp.maximum(lse, l4), l16)
                e1, e4, e16 = jnp.exp(lse - m), jnp.exp(l4 - m), jnp.exp(l16 - m)
                o_ref[rows, :] = (e1 * o + e4 * tok(n4o_ref) + e16 * tok(n16o_ref)) / (e1 + e4 + e16)
            else:
                o_ref[rr, rows, :] = o
                l_ref[rr, rows, :] = lse


def _dil_branch(qkv, masks, batch, seq, dil, others=None):
    sub_len = seq // dil
    tj, rps = DIL_STEP[dil]
    nj = sub_len // tj
    col = lambda cb, rows, jmap: pl.BlockSpec((None, rps, rows, GW), lambda b, r, j: (b, r, jmap(j), cb))
    in_specs = [col(0, tj, lambda j: j), col(1, sub_len, lambda j: 0), col(2, sub_len, lambda j: 0),
                pl.BlockSpec(masks.shape, lambda b, r, j: (0, 0, 0))]
    out_blk = pl.BlockSpec((None, rps, tj, GW), lambda b, r, j: (b, r, j, 0))
    out_sd = jax.ShapeDtypeStruct((batch, dil, sub_len, GW), F32)
    args = [qkv, qkv, qkv, masks]
    scratch = []
    if others is None:
        out_specs, out_shape = [out_blk, out_blk], [out_sd, out_sd]
    else:
        assert dil == 1 and rps == 1
        for d in (4, 16):
            in_specs += [pl.BlockSpec((None, d, tj // d, GW), lambda b, r, j: (b, 0, j, 0))] * 2
        args += list(others)
        out_specs = pl.BlockSpec((tj, GW), lambda b, r, j: (b * nj + j, 0))
        out_shape = jax.ShapeDtypeStruct((batch * seq, GW), F32)
        scratch = [pltpu.VMEM((GW // LANES, tj, LANES), F32)] * 4
    return pl.pallas_call(
        functools.partial(_dil_kernel, sub_len=sub_len, tj=tj, rps=rps, merge=others is not None),
        grid=(batch, dil // rps, nj),
        in_specs=in_specs,
        out_specs=out_specs,
        out_shape=out_shape,
        scratch_shapes=scratch,
        compiler_params=_params(("parallel", "parallel", "arbitrary")),
        name=f"dil_attn_d{dil}",
    )(*args)


def _reorder_w_in_kernel(w_ref, o_ref):
    za = 4 * GW
    qb = za + 2 * GLA_RANK
    xc = qb + QKV
    qd = xc + 2 * GW
    o_ref[0:za, :] = w_ref[0:za, :].astype(BF16)
    o_ref[za:Z_COL0, :] = w_ref[xc:qd, :].astype(BF16)
    o_ref[Z_COL0:Z_COL0 + 2 * GLA_RANK, :] = w_ref[za:qb, :].astype(BF16)
    o_ref[Z_COL0 + 2 * GLA_RANK:PA_COLS, :] = jnp.zeros((LANES - 2 * GLA_RANK, D_MODEL), BF16)
    o_ref[PA_COLS:PA_COLS + QKV, :] = w_ref[qb:xc, :].astype(BF16)
    o_ref[PA_COLS + QKV:P_COLS, :] = w_ref[qd:qd + QKV, :].astype(BF16)


def _reorder_w_in(w_in):
    depth, rows, cols = w_in.shape
    return pl.pallas_call(
        _reorder_w_in_kernel,
        grid=(depth,),
        in_specs=[pl.BlockSpec((None, cols, rows), lambda l: (l, 0, 0))],
        out_specs=pl.BlockSpec((None, P_COLS, rows), lambda l: (l, 0, 0)),
        out_shape=jax.ShapeDtypeStruct((depth, P_COLS, rows), BF16),
        compiler_params=_params(("parallel",)),
        name="w_in_layout",
    )(jnp.swapaxes(w_in, 1, 2))


def _block_diag(w):
    eye = jnp.eye(HEADS, dtype=w.dtype)[:, None, :, None]
    out = w[..., :, :, None, :] * eye
    return out.reshape(w.shape[:-3] + (GW, GW))


def _gate_weights(w_gate):
    per_dir = [jnp.pad(w_gate[:, e], ((0, 0), (e * GLA_RANK, LANES - (e + 1) * GLA_RANK), (0, 0))) for e in (0, 1)]
    return jnp.stack(per_dir, axis=1).astype(BF16)


def _rope_tables(seq):
    pos = jnp.arange(seq, dtype=F32)
    inv_freq = ROPE_THETA ** (-jnp.arange(0, HEAD_DIM, 2, dtype=F32) / HEAD_DIM)
    inv_lane = jnp.tile(inv_freq, 2 * HEADS)
    sign = jnp.tile(jnp.concatenate([-jnp.ones(HEAD_DIM // 2, F32), jnp.ones(HEAD_DIM // 2, F32)]), HEADS)
    ang = pos[:, None] * inv_lane[None, :]
    return jnp.cos(ang), jnp.sin(ang) * sign[None, :]


def kernel(x, mix_norm_pre, mix_norm_post, w_in, gla_w_gate, gla_b_gate, gla_norm, na_rpb, lru_conv_w, lru_conv_b, lru_w_a, lru_b_a, lru_w_x, lru_b_x, lru_lambda, w_out, ffn_norm_pre, ffn_norm_post, ffn_w_in, ffn_w_out):
    batch, seq, d_model = x.shape
    assert d_model == D_MODEL and seq % TB_GLA == 0 and seq % (NA_BLOCK_ROWS * GRID_W) == 0
    assert all(seq % (d * DIL_STEP[d][0]) == 0 and d % DIL_STEP[d][1] == 0 for d in DILATIONS)
    depth = w_in.shape[0]
    n = batch * seq
    xf = x.reshape(n, D_MODEL)
    cos_t, sin_t = _rope_tables(seq)
    masks = _band_masks()
    rows = lambda t: t[..., None, :]
    w = dict(
        mix_pre=rows(mix_norm_pre), mix_post=rows(mix_norm_post), w_in=_reorder_w_in(w_in),
        gla_wg=_gate_weights(gla_w_gate), gla_bg=rows(gla_b_gate), gla_norm=rows(gla_norm),
        nbr_bias=_nbr_bias(na_rpb),
        lru_conv_w=lru_conv_w, lru_conv_b=rows(lru_conv_b),
        lru_wa=_block_diag(lru_w_a).astype(BF16), lru_ba=rows(lru_b_a),
        lru_wx=_block_diag(lru_w_x).astype(BF16), lru_bx=rows(lru_b_x), lru_lam=rows(lru_lambda),
        w_out=w_out.astype(BF16), ffn_pre=rows(ffn_norm_pre), ffn_post=rows(ffn_norm_post),
        ffn_in=ffn_w_in.astype(BF16), ffn_out=ffn_w_out.astype(BF16),
    )

    for l in range(depth):
        pa, pb, d1, d4, d16 = _in_proj(xf, w, l, cos_t, sin_t, batch, seq)

        gla_dirs = _gla(pa, w, l, batch, seq)
        yb = _nbr(pb, w, l, batch, seq)
        lru_dirs = (_lru_dir(pa, w, l, batch, seq, False), _lru_dir(pa, w, l, batch, seq, True))
        o4, l4 = _dil_branch(d4, masks, batch, seq, 4)
        o16, l16 = _dil_branch(d16, masks, batch, seq, 16)
        yd = _dil_branch(d1, masks, batch, seq, 1, others=(o4, l4, o16, l16))

        xf = _tail(pa, gla_dirs, yb, lru_dirs, yd, xf, w, l)
    return xf.reshape(batch, seq, D_MODEL)
```
